```python
import jax, jax.numpy as jnp
from jax import lax
import numpy as np

D_MODEL = 2048
BATCH = 1
SEQ = 8192
DEPTH = 2
DEC_BATCH = 2
DEC_SEQ = 4096
PAST_LEN = 128

GRID_W = 64
D_MIX = D_MODEL
D_CONV = D_MIX // 4
D_ATT = D_MIX // 2
D_LRU = D_MIX - D_CONV - D_ATT
N_CONV_GROUPS = 4
CONV_GROUP = D_CONV // N_CONV_GROUPS
CONV_WIDTH = 31
HEAD_DIM = 128
N_HEADS = D_ATT // HEAD_DIM
WIN_ROWS_MAX = 8
WIN_COLS = 16
N_LRU_BLOCKS = 4
LRU_BLOCK = D_LRU // N_LRU_BLOCKS
LRU_CONV_WIDTH = 4
LRU_C = 8.0
D_FF = 5632
FFN_CONV_WIDTH = 3
D_IN = 2 * D_CONV + 3 * D_ATT + 2 * D_LRU
EPS = 1e-6

kernel_name = 'hybrid_bidir_encoder_conv_na_rglru'


def rmsnorm(x, g):
    xf = x.astype(jnp.float32)
    y = xf * lax.rsqrt(jnp.mean(xf * xf, axis=-1, keepdims=True) + EPS)
    return (y * g.astype(jnp.float32)).astype(x.dtype)


def depthwise_conv(x, w, b, pad_left, pad_right):
    filt = w[:, None, :].astype(x.dtype)
    y = lax.conv_general_dilated(x, filt, window_strides=(1,), padding=((pad_left, pad_right),),
                                 dimension_numbers=('NWC', 'WIO', 'NWC'), feature_group_count=x.shape[-1])
    return y + b.astype(x.dtype)


def conformer_conv(u, dw_w, dw_b, ln_g, ln_b):
    a, gate = jnp.split(u, 2, axis=-1)
    v = a * jax.nn.sigmoid(gate)
    v = depthwise_conv(v, dw_w, dw_b, CONV_WIDTH // 2, CONV_WIDTH // 2)
    B, T, _ = v.shape
    vg = v.astype(jnp.float32).reshape(B, T, N_CONV_GROUPS, CONV_GROUP)
    mu = jnp.mean(vg, axis=-1, keepdims=True)
    var = jnp.mean(jnp.square(vg - mu), axis=-1, keepdims=True)
    vn = ((vg - mu) * lax.rsqrt(var + EPS)).reshape(B, T, D_CONV)
    vn = vn * ln_g.astype(jnp.float32) + ln_b.astype(jnp.float32)
    return jax.nn.silu(vn).astype(u.dtype)


def neighbourhood_attention(q, k, v, rel_bias):
    B, T, _ = q.shape
    rows = T // GRID_W
    kr = min(WIN_ROWS_MAX, rows)
    kg = k.reshape(B, rows, GRID_W, N_HEADS, HEAD_DIM).transpose(0, 3, 1, 2, 4)
    vg = v.reshape(B, rows, GRID_W, N_HEADS, HEAD_DIM).transpose(0, 3, 1, 2, 4)
    qg = q.reshape(B, rows, GRID_W, N_HEADS, HEAD_DIM).transpose(1, 0, 3, 2, 4)
    cols = jnp.arange(GRID_W)
    col_start = jnp.clip(cols - WIN_COLS // 2, 0, GRID_W - WIN_COLS)
    col_idx = col_start[:, None] + jnp.arange(WIN_COLS)[None, :]
    dc = col_idx - cols[:, None] + (WIN_COLS - 1)
    scale = HEAD_DIM ** -0.5

    def row_block(args):
        i, q_row = args
        rs = jnp.clip(i - kr // 2, 0, rows - kr)
        k_rows = lax.dynamic_slice_in_dim(kg, rs, kr, axis=2)
        v_rows = lax.dynamic_slice_in_dim(vg, rs, kr, axis=2)
        k_win = jnp.take(k_rows, col_idx, axis=3)
        v_win = jnp.take(v_rows, col_idx, axis=3)
        dr = rs + jnp.arange(kr) - i + (WIN_ROWS_MAX - 1)
        bias = rel_bias[:, dr][:, :, dc].transpose(0, 2, 1, 3)
        s = jnp.einsum('bhjd,bhrjcd->bhjrc', q_row, k_win).astype(jnp.float32) * scale
        s = s + bias[None].astype(jnp.float32)
        p = jax.nn.softmax(s.reshape(B, N_HEADS, GRID_W, kr * WIN_COLS), axis=-1)
        p = p.reshape(B, N_HEADS, GRID_W, kr, WIN_COLS).astype(v.dtype)
        return jnp.einsum('bhjrc,bhrjcd->bhjd', p, v_win)

    out = lax.map(row_block, (jnp.arange(rows), qg))
    return out.transpose(1, 0, 3, 2, 4).reshape(B, T, D_ATT)


def linear_scan(a, b):
    def combine(c1, c2):
        a1, b1 = c1
        a2, b2 = c2
        return a1 * a2, a2 * b1 + b2
    _, h = lax.associative_scan(combine, (a, b), axis=1)
    return h


def rglru_direction(xc, wa, ba, wx, bx, lam):
    B, T, _ = xc.shape
    xf = xc.astype(jnp.float32)
    xb = xf.reshape(B, T, N_LRU_BLOCKS, LRU_BLOCK)
    r = jax.nn.sigmoid(jnp.einsum('btnk,nkj->btnj', xb, wa.astype(jnp.float32)).reshape(B, T, D_LRU) + ba.astype(jnp.float32))
    ig = jax.nn.sigmoid(jnp.einsum('btnk,nkj->btnj', xb, wx.astype(jnp.float32)).reshape(B, T, D_LRU) + bx.astype(jnp.float32))
    log_a = LRU_C * r * jax.nn.log_sigmoid(lam.astype(jnp.float32))
    a = jnp.exp(log_a)
    b = jnp.sqrt(-jnp.expm1(2.0 * log_a)) * (ig * xf)
    return linear_scan(a, b)


def recurrent_mixer(u, conv_w, conv_b, wa, ba, wx, bx, lam):
    g, xr = jnp.split(u, 2, axis=-1)
    xc = depthwise_conv(xr, conv_w, conv_b, LRU_CONV_WIDTH // 2, LRU_CONV_WIDTH - 1 - LRU_CONV_WIDTH // 2)
    h_f = rglru_direction(xc, wa[0], ba[0], wx[0], bx[0], lam[0])
    h_b = jnp.flip(rglru_direction(jnp.flip(xc, axis=1), wa[1], ba[1], wx[1], bx[1], lam[1]), axis=1)
    return (jax.nn.gelu(g.astype(jnp.float32)) * (h_f + h_b)).astype(u.dtype)


def conv_ffn(h, w_up, conv_w, conv_b, w_down):
    u = h @ w_up
    u = depthwise_conv(u, conv_w, conv_b, FFN_CONV_WIDTH // 2, FFN_CONV_WIDTH // 2)
    gate, val = jnp.split(u, 2, axis=-1)
    return (jax.nn.silu(gate) * val) @ w_down


def trunk(x, norm_mix_g, w_in, conv_dw_w, conv_dw_b, conv_ln_g, conv_ln_b, na_rel_bias,
          lru_conv_w, lru_conv_b, lru_wa, lru_ba, lru_wx, lru_bx, lru_lambda, w_out,
          norm_ffn_g, ffn_w_up, ffn_conv_w, ffn_conv_b, ffn_w_down, final_norm_g):
    splits = [2 * D_CONV, 2 * D_CONV + D_ATT, 2 * D_CONV + 2 * D_ATT, 2 * D_CONV + 3 * D_ATT]
    for l in range(DEPTH):
        h = rmsnorm(x, norm_mix_g[l])
        proj = h @ w_in[l]
        u_conv, q, k, v, u_lru = jnp.split(proj, splits, axis=-1)
        ya = conformer_conv(u_conv, conv_dw_w[l], conv_dw_b[l], conv_ln_g[l], conv_ln_b[l])
        yb = neighbourhood_attention(q, k, v, na_rel_bias[l])
        yc = recurrent_mixer(u_lru, lru_conv_w[l], lru_conv_b[l], lru_wa[l], lru_ba[l],
                             lru_wx[l], lru_bx[l], lru_lambda[l])
        x = x + jnp.concatenate([ya, yb, yc], axis=-1) @ w_out[l]
        h = rmsnorm(x, norm_ffn_g[l])
        x = x + conv_ffn(h, ffn_w_up[l], ffn_conv_w[l], ffn_conv_b[l], ffn_w_down[l])
    return rmsnorm(x, final_norm_g)


def setup_inputs(seed: int = 0) -> dict:
    key = jax.random.key(seed)
    ks = jax.random.split(key, 24)
    f32 = jnp.float32

    def nrm(k, shape, scale):
        return jax.random.normal(k, shape, f32) * scale

    x_prompt = nrm(ks[0], (BATCH, SEQ, D_MODEL), 1.0)
    x_sample = nrm(ks[1], (DEC_BATCH, DEC_SEQ, D_MODEL), 1.0)
    norm_mix_g = 1.0 + nrm(ks[2], (DEPTH, D_MODEL), 0.02)
    w_in = nrm(ks[3], (DEPTH, D_MODEL, D_IN), D_MODEL ** -0.5)
    conv_dw_w = nrm(ks[4], (DEPTH, CONV_WIDTH, D_CONV), CONV_WIDTH ** -0.5)
    conv_dw_b = nrm(ks[5], (DEPTH, D_CONV), 0.02)
    conv_ln_g = 1.0 + nrm(ks[6], (DEPTH, D_CONV), 0.02)
    conv_ln_b = nrm(ks[7], (DEPTH, D_CONV), 0.02)
    na_rel_bias = nrm(ks[8], (DEPTH, N_HEADS, 2 * WIN_ROWS_MAX - 1, 2 * WIN_COLS - 1), 0.1)
    lru_conv_w = nrm(ks[9], (DEPTH, LRU_CONV_WIDTH, D_LRU), LRU_CONV_WIDTH ** -0.5)
    lru_conv_b = nrm(ks[10], (DEPTH, D_LRU), 0.02)
    lru_wa = nrm(ks[11], (DEPTH, 2, N_LRU_BLOCKS, LRU_BLOCK, LRU_BLOCK), LRU_BLOCK ** -0.5)
    lru_ba = nrm(ks[12], (DEPTH, 2, D_LRU), 0.02)
    lru_wx = nrm(ks[13], (DEPTH, 2, N_LRU_BLOCKS, LRU_BLOCK, LRU_BLOCK), LRU_BLOCK ** -0.5)
    lru_bx = nrm(ks[14], (DEPTH, 2, D_LRU), 0.02)
    a_init = jax.random.uniform(ks[15], (DEPTH, 2, D_LRU), f32, 0.9, 0.999)
    s = a_init ** (1.0 / LRU_C)
    lru_lambda = jnp.log(s) - jnp.log1p(-s)
    w_out = nrm(ks[16], (DEPTH, D_MIX, D_MODEL), D_MIX ** -0.5)
    norm_ffn_g = 1.0 + nrm(ks[17], (DEPTH, D_MODEL), 0.02)
    ffn_w_up = nrm(ks[18], (DEPTH, D_MODEL, 2 * D_FF), D_MODEL ** -0.5)
    ffn_conv_w = nrm(ks[19], (DEPTH, FFN_CONV_WIDTH, 2 * D_FF), FFN_CONV_WIDTH ** -0.5)
    ffn_conv_b = nrm(ks[20], (DEPTH, 2 * D_FF), 0.02)
    ffn_w_down = nrm(ks[21], (DEPTH, D_FF, D_MODEL), D_FF ** -0.5)
    final_norm_g = 1.0 + nrm(ks[22], (D_MODEL,), 0.02)
    return {'x_prompt': x_prompt, 'x_sample': x_sample, 'norm_mix_g': norm_mix_g, 'w_in': w_in,
            'conv_dw_w': conv_dw_w, 'conv_dw_b': conv_dw_b, 'conv_ln_g': conv_ln_g, 'conv_ln_b': conv_ln_b,
            'na_rel_bias': na_rel_bias, 'lru_conv_w': lru_conv_w, 'lru_conv_b': lru_conv_b,
            'lru_wa': lru_wa, 'lru_ba': lru_ba, 'lru_wx': lru_wx, 'lru_bx': lru_bx, 'lru_lambda': lru_lambda,
            'w_out': w_out, 'norm_ffn_g': norm_ffn_g, 'ffn_w_up': ffn_w_up, 'ffn_conv_w': ffn_conv_w,
            'ffn_conv_b': ffn_conv_b, 'ffn_w_down': ffn_w_down, 'final_norm_g': final_norm_g}


def reference(x_prompt, x_sample, norm_mix_g, w_in, conv_dw_w, conv_dw_b, conv_ln_g, conv_ln_b,
              na_rel_bias, lru_conv_w, lru_conv_b, lru_wa, lru_ba, lru_wx, lru_bx, lru_lambda,
              w_out, norm_ffn_g, ffn_w_up, ffn_conv_w, ffn_conv_b, ffn_w_down, final_norm_g):
    y_prompt = trunk(x_prompt, norm_mix_g, w_in, conv_dw_w, conv_dw_b, conv_ln_g, conv_ln_b, na_rel_bias,
                     lru_conv_w, lru_conv_b, lru_wa, lru_ba, lru_wx, lru_bx, lru_lambda, w_out,
                     norm_ffn_g, ffn_w_up, ffn_conv_w, ffn_conv_b, ffn_w_down, final_norm_g)
    y_sample = trunk(x_sample, norm_mix_g, w_in, conv_dw_w, conv_dw_b, conv_ln_g, conv_ln_b, na_rel_bias,
                     lru_conv_w, lru_conv_b, lru_wa, lru_ba, lru_wx, lru_bx, lru_lambda, w_out,
                     norm_ffn_g, ffn_w_up, ffn_conv_w, ffn_conv_b, ffn_w_down, final_norm_g)
    return (y_prompt, y_sample)
```

```python
import functools

import numpy as np
import jax
import jax.numpy as jnp
from jax import lax
from jax.experimental import pallas as pl
from jax.experimental.pallas import tpu as pltpu

F32 = jnp.float32
BF16 = jnp.bfloat16

D_MODEL = 2048
GRID_W = 64
D_CONV = 512
D_ATT = 1024
D_LRU = 512
N_CONV_GROUPS = 4
CONV_GROUP = 128
CONV_WIDTH = 31
HEAD_DIM = 128
N_HEADS = 8
WIN_ROWS = 8
WIN_COLS = 16
N_LRU_BLOCKS = 4
LRU_BLOCK = 128
LRU_CONV_WIDTH = 4
LRU_C = 8.0
D_FF = 5632
D_IN = 5120
EPS = 1e-6
NEG_BIG = -1e30

LANES = 128
SUBLANES_F32 = 8
SUBLANES_BF16 = 16
VMEM_LIMIT_BYTES = 56 * 1024 * 1024

TM_PROJ = 1024
TN_PROJ = 1024
TM_SEQ = 512
TF_FFN = 512
CONV_HALO = 16
LRU_HALO = 8
FFN_HALO = 16
CONV_CHUNK = 32
Q_ROWS = 8


def _cparams(semantics):
    return pltpu.CompilerParams(dimension_semantics=semantics,
                                vmem_limit_bytes=VMEM_LIMIT_BYTES)


def _tile_flags(seqs, n_tokens, tm):
    starts = {s for s, _ in seqs}
    ends = {e for _, e in seqs}
    nt = n_tokens // tm
    first = np.array([1 if i * tm in starts else 0 for i in range(nt)], np.int32)
    last = np.array([1 if (i + 1) * tm in ends else 0 for i in range(nt)], np.int32)
    return jnp.asarray(first), jnp.asarray(last)


def _rms_scale(x):
    return lax.rsqrt(jnp.mean(x * x, axis=-1, keepdims=True) + EPS)


def _inproj_kernel(x_ref, g_ref, w_ref, oa_ref, oqkv_ref, ol_ref, h_ref):
    j = pl.program_id(1)

    @pl.when(j == 0)
    def _():
        x = x_ref[...]
        h_ref[...] = (x * _rms_scale(x) * g_ref[...]).astype(BF16)

    r = jnp.dot(h_ref[...], w_ref[...], preferred_element_type=F32)

    @pl.when(j == 0)
    def _():
        oa_ref[...] = r

    @pl.when((j >= 1) & (j <= 3))
    def _():
        oqkv_ref[...] = r.astype(BF16)

    @pl.when(j == 4)
    def _():
        ol_ref[...] = r


def _inproj(x, g, w_bf16):
    n = x.shape[0]
    tm, tn = TM_PROJ, TN_PROJ
    assert n % tm == 0 and D_IN == 5 * tn and 2 * D_CONV == tn and D_ATT == tn
    return pl.pallas_call(
        _inproj_kernel,
        grid=(n // tm, D_IN // tn),
        in_specs=[
            pl.BlockSpec((tm, D_MODEL), lambda i, j: (i, 0)),
            pl.BlockSpec((1, D_MODEL), lambda i, j: (0, 0)),
            pl.BlockSpec((D_MODEL, tn), lambda i, j: (0, j)),
        ],
        out_specs=[
            pl.BlockSpec((tm, tn), lambda i, j: (i, 0)),
            pl.BlockSpec((tm, tn), lambda i, j: (i, jnp.clip(j - 1, 0, 2))),
            pl.BlockSpec((tm, tn), lambda i, j: (i, 0)),
        ],
        out_shape=[
            jax.ShapeDtypeStruct((n, 2 * D_CONV), F32),
            jax.ShapeDtypeStruct((n, 3 * D_ATT), BF16),
            jax.ShapeDtypeStruct((n, 2 * D_LRU), F32),
        ],
        scratch_shapes=[pltpu.VMEM((tm, D_MODEL), BF16)],
        compiler_params=_cparams(("parallel", "arbitrary")),
        name="inproj",
    )(x, g.reshape(1, D_MODEL), w_bf16)


def _conv_kernel(first_ref, last_ref, am_ref, gm_ref, ap_ref, gp_ref, an_ref, gn_ref,
                 w_ref, b_ref, lg_ref, lb_ref, o_ref, vext_ref, *, tm):
    i = pl.program_id(0)
    halo = CONV_HALO
    vp = ap_ref[...] * jax.nn.sigmoid(gp_ref[...])
    vn = an_ref[...] * jax.nn.sigmoid(gn_ref[...])
    vext_ref[0:halo, :] = jnp.where(first_ref[i] == 1, 0.0, vp)
    vext_ref[halo:halo + tm, :] = am_ref[...] * jax.nn.sigmoid(gm_ref[...])
    vext_ref[halo + tm:2 * halo + tm, :] = jnp.where(last_ref[i] == 1, 0.0, vn)

    ch = CONV_CHUNK
    lead = halo - CONV_WIDTH // 2

    def body(c, carry):
        r0 = pl.multiple_of(c * ch, ch)
        win = vext_ref[pl.ds(r0, ch + 2 * halo), :]
        acc = jnp.broadcast_to(b_ref[...], (ch, D_CONV))
        for k in range(CONV_WIDTH):
            acc = acc + w_ref[k:k + 1, :] * win[lead + k:lead + k + ch, :]
        outs = []
        for gi in range(N_CONV_GROUPS):
            xg = acc[:, gi * CONV_GROUP:(gi + 1) * CONV_GROUP]
            mu = jnp.mean(xg, axis=-1, keepdims=True)
            d = xg - mu
            var = jnp.mean(d * d, axis=-1, keepdims=True)
            outs.append(d * lax.rsqrt(var + EPS))
        y = jnp.concatenate(outs, axis=1) * lg_ref[...] + lb_ref[...]
        o_ref[pl.ds(r0, ch), :] = (y * jax.nn.sigmoid(y)).astype(o_ref.dtype)
        return carry

    lax.fori_loop(0, tm // ch, body, 0)


def _conformer_conv(u_conv, seqs, dw_w, dw_b, ln_g, ln_b):
    n = u_conv.shape[0]
    tm, halo = TM_SEQ, CONV_HALO
    nt = n // tm
    hb = tm // halo
    nhb = n // halo
    first, last = _tile_flags(seqs, n, tm)

    def main(col):
        return pl.BlockSpec((tm, D_CONV), lambda i, f, l: (i, col))

    def prev(col):
        return pl.BlockSpec((halo, D_CONV), lambda i, f, l: (jnp.maximum(i * hb - 1, 0), col))

    def nxt(col):
        return pl.BlockSpec((halo, D_CONV), lambda i, f, l: (jnp.minimum((i + 1) * hb, nhb - 1), col))

    def const(shape):
        return pl.BlockSpec(shape, lambda i, f, l: (0, 0))

    grid_spec = pltpu.PrefetchScalarGridSpec(
        num_scalar_prefetch=2,
        grid=(nt,),
        in_specs=[main(0), main(1), prev(0), prev(1), nxt(0), nxt(1),
                  const((CONV_WIDTH, D_CONV)), const((1, D_CONV)), const((1, D_CONV)), const((1, D_CONV))],
        out_specs=pl.BlockSpec((tm, D_CONV), lambda i, f, l: (i, 0)),
        scratch_shapes=[pltpu.VMEM((tm + 2 * halo, D_CONV), F32)],
    )
    return pl.pallas_call(
        functools.partial(_conv_kernel, tm=tm),
        grid_spec=grid_spec,
        out_shape=jax.ShapeDtypeStruct((n, D_CONV), BF16),
        compiler_params=_cparams(("parallel",)),
        name="conformer_conv",
    )(first, last, u_conv, u_conv, u_conv, u_conv, u_conv, u_conv,
      dw_w, dw_b.reshape(1, D_CONV), ln_g.reshape(1, D_CONV), ln_b.reshape(1, D_CONV))


N_BIAS_PAIRS = 2 * WIN_ROWS - 2


def _bias_pairs_kernel(rb_ref, o_ref):
    h = pl.program_id(0)
    n_dc = 2 * WIN_COLS - 1
    n_dr = 2 * WIN_ROWS - 1
    jj = lax.broadcasted_iota(jnp.int32, (GRID_W, 2 * GRID_W), 0)
    ll = lax.broadcasted_iota(jnp.int32, (GRID_W, 2 * GRID_W), 1)
    cc = ll & (GRID_W - 1)
    hi = ll >= GRID_W
    dc = cc - jj + (WIN_COLS - 1)
    cs = jnp.clip(jj - WIN_COLS // 2, 0, GRID_W - WIN_COLS)
    in_win = (cc >= cs) & (cc < cs + WIN_COLS)
    for a in range(N_BIAS_PAIRS):
        acc = jnp.zeros((GRID_W, 2 * GRID_W), F32)
        for d in range(n_dc):
            lo_v = rb_ref[(h * n_dr + a) * n_dc + d]
            hi_v = rb_ref[(h * n_dr + a + 1) * n_dc + d]
            acc = jnp.where(dc == d, jnp.where(hi, hi_v, lo_v), acc)
        o_ref[0, a] = jnp.where(in_win, acc, NEG_BIG)


def _bias_pairs(rel_bias):
    return pl.pallas_call(
        _bias_pairs_kernel,
        grid=(N_HEADS,),
        in_specs=[pl.BlockSpec(memory_space=pltpu.SMEM)],
        out_specs=pl.BlockSpec((1, N_BIAS_PAIRS, GRID_W, 2 * GRID_W), lambda h: (h, 0, 0, 0)),
        out_shape=jax.ShapeDtypeStruct((N_HEADS, N_BIAS_PAIRS, GRID_W, 2 * GRID_W), F32),
        compiler_params=_cparams(("parallel",)),
        name="na_bias_pairs",
    )(rel_bias.reshape(-1))


def _na_kernel(rs_ref, d0_ref, q_ref, k_ref, v_ref, p_ref, o_ref):
    blk = pl.program_id(1)
    scale = HEAD_DIM ** -0.5
    n_keys = WIN_ROWS * GRID_W
    for r in range(Q_ROWS):
        g = blk * Q_ROWS + r
        start = pl.multiple_of(rs_ref[g] * GRID_W, GRID_W)
        d0 = d0_ref[g]
        q = q_ref[r * GRID_W:(r + 1) * GRID_W, :]
        k = k_ref[pl.ds(start, n_keys), :]
        v = v_ref[pl.ds(start, n_keys), :]
        s = lax.dot_general(q, k, (((1,), (1,)), ((), ())), preferred_element_type=F32) * scale
        bias = jnp.concatenate([p_ref[0, d0 + 2 * t] for t in range(WIN_ROWS // 2)], axis=1)
        s = s + bias
        m = jnp.max(s, axis=-1, keepdims=True)
        p = jnp.exp(s - m)
        l = jnp.sum(p, axis=-1, keepdims=True)
        o = jnp.dot(p.astype(BF16), v, preferred_element_type=F32) / l
        o_ref[r * GRID_W:(r + 1) * GRID_W, :] = o.astype(o_ref.dtype)


def _na_rows(seqs):
    rs, d0 = [], []
    for s, e in seqs:
        r0, r1 = s // GRID_W, e // GRID_W
        assert r1 - r0 >= WIN_ROWS
        for g in range(r0, r1):
            st = min(max(g - WIN_ROWS // 2, r0), r1 - WIN_ROWS)
            rs.append(st)
            d0.append(st - g + WIN_ROWS - 1)
    return jnp.asarray(np.array(rs, np.int32)), jnp.asarray(np.array(d0, np.int32))


def _neighbourhood_attention(qkv, seqs, bias_pairs):
    n = qkv.shape[0]
    tq = Q_ROWS * GRID_W
    assert n % tq == 0 and all((e - s) % tq == 0 for s, e in seqs)
    rs, d0 = _na_rows(seqs)
    grid_spec = pltpu.PrefetchScalarGridSpec(
        num_scalar_prefetch=2,
        grid=(N_HEADS, n // tq),
        in_specs=[
            pl.BlockSpec((tq, HEAD_DIM), lambda h, b, rs, d0: (b, h)),
            pl.BlockSpec((n, HEAD_DIM), lambda h, b, rs, d0: (0, N_HEADS + h)),
            pl.BlockSpec((n, HEAD_DIM), lambda h, b, rs, d0: (0, 2 * N_HEADS + h)),
            pl.BlockSpec((1, N_BIAS_PAIRS, GRID_W, 2 * GRID_W), lambda h, b, rs, d0: (h, 0, 0, 0)),
        ],
        out_specs=pl.BlockSpec((tq, HEAD_DIM), lambda h, b, rs, d0: (b, h)),
    )
    return pl.pallas_call(
        _na_kernel,
        grid_spec=grid_spec,
        out_shape=jax.ShapeDtypeStruct((n, D_ATT), BF16),
        compiler_params=_cparams(("parallel", "arbitrary")),
        name="neighbourhood_attention",
    )(rs, d0, qkv, qkv, qkv, bias_pairs)


def _log_sigmoid(x):
    return jnp.minimum(x, 0.0) - jnp.log1p(jnp.exp(-jnp.abs(x)))


def _chunk_scan(a, b, reverse):
    rows = a.shape[0]
    row = lax.broadcasted_iota(jnp.int32, a.shape, 0)
    d = 1
    while d < rows:
        if reverse:
            ok = row < rows - d
            a_sh = jnp.where(ok, pltpu.roll(a, rows - d, 0), 1.0)
            b_sh = jnp.where(ok, pltpu.roll(b, rows - d, 0), 0.0)
        else:
            ok = row >= d
            a_sh = jnp.where(ok, pltpu.roll(a, d, 0), 1.0)
            b_sh = jnp.where(ok, pltpu.roll(b, d, 0), 0.0)
        b = a * b_sh + b
        a = a * a_sh
        d *= 2
    return a, b


def _lru_kernel(first_ref, last_ref,
                xmf_ref, xpf_ref, xnf_ref, xmb_ref, xpb_ref, xnb_ref,
                cw_ref, cb_ref, w_ref, ba_ref, bx_ref, lam_ref,
                hf_ref, hb_ref,
                xext_ref, a_ref, b_ref, carry_ref, *, tm, nt):
    i = pl.program_id(0)
    halo = LRU_HALO
    lead = halo - LRU_CONV_WIDTH // 2

    @pl.when(i == 0)
    def _():
        carry_ref[...] = jnp.zeros_like(carry_ref)

    def gates(d, tile, xm_ref, xp_ref, xn_ref):
        xext_ref[0:halo, :] = jnp.where(first_ref[tile] == 1, 0.0, xp_ref[...])
        xext_ref[halo:halo + tm, :] = xm_ref[...]
        xext_ref[halo + tm:2 * halo + tm, :] = jnp.where(last_ref[tile] == 1, 0.0, xn_ref[...])
        xc = jnp.broadcast_to(cb_ref[...], (tm, D_LRU))
        for k in range(LRU_CONV_WIDTH):
            xc = xc + cw_ref[k:k + 1, :] * xext_ref[lead + k:lead + k + tm, :]
        log_s = _log_sigmoid(lam_ref[d])
        for nb in range(N_LRU_BLOCKS):
            sl = slice(nb * LRU_BLOCK, (nb + 1) * LRU_BLOCK)
            xb = xc[:, sl]
            z = jnp.dot(xb.astype(BF16), w_ref[d, nb], preferred_element_type=F32)
            r = jax.nn.sigmoid(z[:, :LRU_BLOCK] + ba_ref[d][:, sl])
            ig = jax.nn.sigmoid(z[:, LRU_BLOCK:] + bx_ref[d][:, sl])
            log_a = LRU_C * r * log_s[:, sl]
            a = jnp.exp(log_a)
            a_ref[d, :, sl] = a
            b_ref[d, :, sl] = jnp.sqrt(-jnp.tanh(log_a) * (1.0 + a * a)) * (ig * xb)

    gates(0, i, xmf_ref, xpf_ref, xnf_ref)
    gates(1, nt - 1 - i, xmb_ref, xpb_ref, xnb_ref)

    rows = SUBLANES_F32
    nch = tm // rows
    cf0 = jnp.where(first_ref[i] == 1, 0.0, carry_ref[0:1, :])
    cb0 = jnp.where(last_ref[nt - 1 - i] == 1, 0.0, carry_ref[1:2, :])

    def body(c, carry):
        cf, cb = carry
        rf = pl.multiple_of(c * rows, rows)
        rb = pl.multiple_of((nch - 1 - c) * rows, rows)
        af, hf = _chunk_scan(a_ref[0, pl.ds(rf, rows), :], b_ref[0, pl.ds(rf, rows), :], False)
        ab, hb = _chunk_scan(a_ref[1, pl.ds(rb, rows), :], b_ref[1, pl.ds(rb, rows), :], True)
        hf = hf + af * cf
        hb = hb + ab * cb
        hf_ref[pl.ds(rf, rows), :] = hf
        hb_ref[pl.ds(rb, rows), :] = hb
        return hf[rows - 1:rows, :], hb[0:1, :]

    cf, cb = lax.fori_loop(0, nch, body, (cf0, cb0))
    carry_ref[0:1, :] = cf
    carry_ref[1:2, :] = cb


def _rglru(u_lru, seqs, conv_w, conv_b, w_gates, ba, bx, lam):
    n = u_lru.shape[0]
    tm, halo = TM_SEQ, LRU_HALO
    nt = n // tm
    hb = tm // halo
    nhb = n // halo
    first, last = _tile_flags(seqs, n, tm)

    def tile_of(i, rev):
        return nt - 1 - i if rev else i

    def main(rev):
        return pl.BlockSpec((tm, D_LRU), lambda i, f, l: (tile_of(i, rev), 1))

    def prev(rev):
        return pl.BlockSpec((halo, D_LRU), lambda i, f, l: (jnp.maximum(tile_of(i, rev) * hb - 1, 0), 1))

    def nxt(rev):
        return pl.BlockSpec((halo, D_LRU), lambda i, f, l: (jnp.minimum((tile_of(i, rev) + 1) * hb, nhb - 1), 1))

    def const(shape):
        return pl.BlockSpec(shape, lambda i, f, l: (0,) * len(shape))

    grid_spec = pltpu.PrefetchScalarGridSpec(
        num_scalar_prefetch=2,
        grid=(nt,),
        in_specs=[main(False), prev(False), nxt(False), main(True), prev(True), nxt(True),
                  const((LRU_CONV_WIDTH, D_LRU)), const((1, D_LRU)),
                  const((2, N_LRU_BLOCKS, LRU_BLOCK, 2 * LRU_BLOCK)),
                  const((2, 1, D_LRU)), const((2, 1, D_LRU)), const((2, 1, D_LRU))],
        out_specs=[pl.BlockSpec((tm, D_LRU), lambda i, f, l: (i, 0)),
                   pl.BlockSpec((tm, D_LRU), lambda i, f, l: (nt - 1 - i, 0))],
        scratch_shapes=[pltpu.VMEM((tm + 2 * halo, D_LRU), F32),
                        pltpu.VMEM((2, tm, D_LRU), F32),
                        pltpu.VMEM((2, tm, D_LRU), F32),
                        pltpu.VMEM((SUBLANES_F32, D_LRU), F32)],
    )
    return pl.pallas_call(
        functools.partial(_lru_kernel, tm=tm, nt=nt),
        grid_spec=grid_spec,
        out_shape=[jax.ShapeDtypeStruct((n, D_LRU), F32), jax.ShapeDtypeStruct((n, D_LRU), F32)],
        compiler_params=_cparams(("arbitrary",)),
        name="rglru",
    )(first, last, u_lru, u_lru, u_lru, u_lru, u_lru, u_lru,
      conv_w, conv_b.reshape(1, D_LRU), w_gates,
      ba.reshape(2, 1, D_LRU), bx.reshape(2, 1, D_LRU), lam.reshape(2, 1, D_LRU))


def _outproj_kernel(x_ref, ya_ref, yb_ref, g_ref, hf_ref, hb_ref, w_ref, gn_ref, x1_ref, h2_ref):
    yc = (jax.nn.gelu(g_ref[...], approximate=True) * (hf_ref[...] + hb_ref[...])).astype(BF16)
    acc = jnp.dot(ya_ref[...], w_ref[0:D_CONV, :], preferred_element_type=F32)
    acc = acc + jnp.dot(yb_ref[...], w_ref[D_CONV:D_CONV + D_ATT, :], preferred_element_type=F32)
    acc = acc + jnp.dot(yc, w_ref[D_CONV + D_ATT:, :], preferred_element_type=F32)
    x1 = x_ref[...] + acc
    x1_ref[...] = x1
    h2_ref[...] = (x1 * _rms_scale(x1) * gn_ref[...]).astype(BF16)


def _outproj(x, ya, yb, u_lru, hf, hb, w_bf16, g_ffn):
    n = x.shape[0]
    tm = TM_SEQ

    def rows(width, col=0):
        return pl.BlockSpec((tm, width), lambda i: (i, col))

    return pl.pallas_call(
        _outproj_kernel,
        grid=(n // tm,),
        in_specs=[rows(D_MODEL), rows(D_CONV), rows(D_ATT), rows(D_LRU, 0), rows(D_LRU), rows(D_LRU),
                  pl.BlockSpec((D_MODEL, D_MODEL), lambda i: (0, 0)),
                  pl.BlockSpec((1, D_MODEL), lambda i: (0, 0))],
        out_specs=[rows(D_MODEL), rows(D_MODEL)],
        out_shape=[jax.ShapeDtypeStruct((n, D_MODEL), F32), jax.ShapeDtypeStruct((n, D_MODEL), BF16)],
        compiler_params=_cparams(("parallel",)),
        name="outproj",
    )(x, ya, yb, u_lru, hf, hb, w_bf16, g_ffn.reshape(1, D_MODEL))


def _ffn_kernel(first_ref, last_ref, hm_ref, hp_ref, hn_ref, x1_ref,
                wg_ref, wv_ref, cwg_ref, cwv_ref, cbg_ref, cbv_ref, wd_ref, fg_ref,
                o_ref, hext_ref, acc_ref, *, tm, final_norm):
    i = pl.program_id(0)
    j = pl.program_id(1)
    halo = FFN_HALO

    @pl.when(j == 0)
    def _():
        hext_ref[0:halo, :] = jnp.where(first_ref[i] == 1, jnp.zeros_like(hp_ref), hp_ref[...])
        hext_ref[halo:halo + tm, :] = hm_ref[...]
        hext_ref[halo + tm:2 * halo + tm, :] = jnp.where(last_ref[i] == 1, jnp.zeros_like(hn_ref), hn_ref[...])
        acc_ref[...] = jnp.zeros_like(acc_ref)

    hext = hext_ref[...]

    def conv3(w_ref, cw_ref, cb_ref):
        u = jnp.dot(hext, w_ref[...], preferred_element_type=F32)
        return (cw_ref[0:1, :] * u[halo - 1:halo - 1 + tm, :]
                + cw_ref[1:2, :] * u[halo:halo + tm, :]
                + cw_ref[2:3, :] * u[halo + 1:halo + 1 + tm, :]
                + cb_ref[...])

    gate = conv3(wg_ref, cwg_ref, cbg_ref)
    val = conv3(wv_ref, cwv_ref, cbv_ref)
    act = (gate * jax.nn.sigmoid(gate) * val).astype(BF16)
    acc_ref[...] += jnp.dot(act, wd_ref[...], preferred_element_type=F32)

    @pl.when(j == pl.num_programs(1) - 1)
    def _():
        x2 = x1_ref[...] + acc_ref[...]
        if final_norm:
            x2 = x2 * _rms_scale(x2) * fg_ref[...]
        o_ref[...] = x2


def _ffn(h2, x1, seqs, w_up_bf16, conv_w, conv_b, w_down_bf16, final_g, final_norm):
    n = h2.shape[0]
    tm, tf, halo = TM_SEQ, TF_FFN, FFN_HALO
    nt = n // tm
    nf = D_FF // tf
    hb = tm // halo
    nhb = n // halo
    first, last = _tile_flags(seqs, n, tm)
    cb = conv_b.reshape(1, 2 * D_FF)
    grid_spec = pltpu.PrefetchScalarGridSpec(
        num_scalar_prefetch=2,
        grid=(nt, nf),
        in_specs=[
            pl.BlockSpec((tm, D_MODEL), lambda i, j, f, l: (i, 0)),
            pl.BlockSpec((halo, D_MODEL), lambda i, j, f, l: (jnp.maximum(i * hb - 1, 0), 0)),
            pl.BlockSpec((halo, D_MODEL), lambda i, j, f, l: (jnp.minimum((i + 1) * hb, nhb - 1), 0)),
            pl.BlockSpec((tm, D_MODEL), lambda i, j, f, l: (i, 0)),
            pl.BlockSpec((D_MODEL, tf), lambda i, j, f, l: (0, j)),
            pl.BlockSpec((D_MODEL, tf), lambda i, j, f, l: (0, nf + j)),
            pl.BlockSpec((3, tf), lambda i, j, f, l: (0, j)),
            pl.BlockSpec((3, tf), lambda i, j, f, l: (0, nf + j)),
            pl.BlockSpec((1, tf), lambda i, j, f, l: (0, j)),
            pl.BlockSpec((1, tf), lambda i, j, f, l: (0, nf + j)),
            pl.BlockSpec((tf, D_MODEL), lambda i, j, f, l: (j, 0)),
            pl.BlockSpec((1, D_MODEL), lambda i, j, f, l: (0, 0)),
        ],
        out_specs=pl.BlockSpec((tm, D_MODEL), lambda i, j, f, l: (i, 0)),
        scratch_shapes=[pltpu.VMEM((tm + 2 * halo, D_MODEL), BF16),
                        pltpu.VMEM((tm, D_MODEL), F32)],
    )
    return pl.pallas_call(
        functools.partial(_ffn_kernel, tm=tm, final_norm=final_norm),
        grid_spec=grid_spec,
        out_shape=jax.ShapeDtypeStruct((n, D_MODEL), F32),
        compiler_params=_cparams(("parallel", "arbitrary")),
        name="ffn_final" if final_norm else "ffn",
    )(first, last, h2, h2, h2, x1, w_up_bf16, w_up_bf16, conv_w, conv_w, cb, cb,
      w_down_bf16, final_g.reshape(1, D_MODEL))


def _trunk(x, seqs, norm_mix_g, w_in, conv_dw_w, conv_dw_b, conv_ln_g, conv_ln_b, na_rel_bias,
           lru_conv_w, lru_conv_b, lru_wa, lru_ba, lru_wx, lru_bx, lru_lambda, w_out,
           norm_ffn_g, ffn_w_up, ffn_conv_w, ffn_conv_b, ffn_w_down, final_norm_g):
    depth = w_in.shape[0]
    for l in range(depth):
        u_conv, qkv, u_lru = _inproj(x, norm_mix_g[l], w_in[l].astype(BF16))
        ya = _conformer_conv(u_conv, seqs, conv_dw_w[l], conv_dw_b[l], conv_ln_g[l], conv_ln_b[l])
        yb = _neighbourhood_attention(qkv, seqs, _bias_pairs(na_rel_bias[l]))
        w_gates = jnp.concatenate([lru_wa[l], lru_wx[l]], axis=-1).astype(BF16)
        hf, hb = _rglru(u_lru, seqs, lru_conv_w[l], lru_conv_b[l], w_gates, lru_ba[l], lru_bx[l], lru_lambda[l])
        x1, h2 = _outproj(x, ya, yb, u_lru, hf, hb, w_out[l].astype(BF16), norm_ffn_g[l])
        x = _ffn(h2, x1, seqs, ffn_w_up[l].astype(BF16), ffn_conv_w[l], ffn_conv_b[l],
                 ffn_w_down[l].astype(BF16), final_norm_g, final_norm=(l == depth - 1))
    return x


def kernel(x_prompt, x_sample, norm_mix_g, w_in, conv_dw_w, conv_dw_b, conv_ln_g, conv_ln_b, na_rel_bias, lru_conv_w, lru_conv_b, lru_wa, lru_ba, lru_wx, lru_bx, lru_lambda, w_out, norm_ffn_g, ffn_w_up, ffn_conv_w, ffn_conv_b, ffn_w_down, final_norm_g):
    seqs = []
    pos = 0
    for arr in (x_prompt, x_sample):
        for _ in range(arr.shape[0]):
            seqs.append((pos, pos + arr.shape[1]))
            pos += arr.shape[1]
    x = jnp.concatenate([x_prompt.reshape(-1, D_MODEL), x_sample.reshape(-1, D_MODEL)], axis=0)
    y = _trunk(x, tuple(seqs), norm_mix_g, w_in, conv_dw_w, conv_dw_b, conv_ln_g, conv_ln_b, na_rel_bias,
               lru_conv_w, lru_conv_b, lru_wa, lru_ba, lru_wx, lru_bx, lru_lambda, w_out,
               norm_ffn_g, ffn_w_up, ffn_conv_w, ffn_conv_b, ffn_w_down, final_norm_g)
    n_prompt = x_prompt.shape[0] * x_prompt.shape[1]
    return (y[:n_prompt].reshape(x_prompt.shape), y[n_prompt:].reshape(x_sample.shape))
```

```python
import functools

import numpy as np
import jax
import jax.numpy as jnp
from jax import lax
from jax.experimental import pallas as pl
from jax.experimental.pallas import tpu as pltpu

F32 = jnp.float32
BF16 = jnp.bfloat16

D_MODEL = 2048
GRID_W = 64
D_CONV = 512
D_ATT = 1024
D_LRU = 512
N_CONV_GROUPS = 4
CONV_GROUP = 128
CONV_WIDTH = 31
HEAD_DIM = 128
N_HEADS = 8
WIN_ROWS = 8
WIN_COLS = 16
N_LRU_BLOCKS = 4
LRU_BLOCK = 128
LRU_CONV_WIDTH = 4
LRU_C = 8.0
D_FF = 5632
D_IN = 5120
EPS = 1e-6
NEG_BIG = -1e30

LANES = 128
SUBLANES_F32 = 8
SUBLANES_BF16 = 16
VMEM_LIMIT_BYTES = 56 * 1024 * 1024

TM_PROJ = 1024
TN_PROJ = 1024
TM_SEQ = 512
TF_FFN = 512
CONV_HALO = 16
LRU_HALO = 8
FFN_HALO = 16
FFN_ACT_ROWS = 32
CONV_CHUNK = 128
Q_ROWS = 8


def _cparams(semantics):
    return pltpu.CompilerParams(dimension_semantics=semantics,
                                vmem_limit_bytes=VMEM_LIMIT_BYTES)


def _tile_flags(seqs, n_tokens, tm):
    starts = {s for s, _ in seqs}
    ends = {e for _, e in seqs}
    nt = n_tokens // tm
    first = np.array([1 if i * tm in starts else 0 for i in range(nt)], np.int32)
    last = np.array([1 if (i + 1) * tm in ends else 0 for i in range(nt)], np.int32)
    return jnp.asarray(first), jnp.asarray(last)


def _rms_scale(x):
    return lax.rsqrt(jnp.mean(x * x, axis=-1, keepdims=True) + EPS)


def _inproj_kernel(x_ref, g_ref, w_ref, ov_ref, oqkv_ref, ol_ref, h_ref):
    j = pl.program_id(1)

    @pl.when(j == 0)
    def _():
        x = x_ref[...]
        h_ref[...] = (x * _rms_scale(x) * g_ref[...]).astype(BF16)

    r = jnp.dot(h_ref[...], w_ref[...], preferred_element_type=F32)

    @pl.when(j == 0)
    def _():
        ov_ref[...] = r[:, :D_CONV] * jax.nn.sigmoid(r[:, D_CONV:])

    @pl.when((j >= 1) & (j <= 3))
    def _():
        oqkv_ref[...] = r.astype(BF16)

    @pl.when(j == 4)
    def _():
        ol_ref[...] = r


def _inproj(x, g, w_bf16, layer):
    n = x.shape[0]
    tm, tn = TM_PROJ, TN_PROJ
    assert n % tm == 0 and D_IN == 5 * tn and 2 * D_CONV == tn and D_ATT == tn
    return pl.pallas_call(
        _inproj_kernel,
        grid=(n // tm, D_IN // tn),
        in_specs=[
            pl.BlockSpec((tm, D_MODEL), lambda i, j: (i, 0)),
            pl.BlockSpec((1, D_MODEL), lambda i, j: (0, 0)),
            pl.BlockSpec((None, D_MODEL, tn), lambda i, j: (layer, 0, j)),
        ],
        out_specs=[
            pl.BlockSpec((tm, D_CONV), lambda i, j: (i, 0)),
            pl.BlockSpec((tm, tn), lambda i, j: (i, jnp.clip(j - 1, 0, 2))),
            pl.BlockSpec((tm, tn), lambda i, j: (i, 0)),
        ],
        out_shape=[
            jax.ShapeDtypeStruct((n, D_CONV), F32),
            jax.ShapeDtypeStruct((n, 3 * D_ATT), BF16),
            jax.ShapeDtypeStruct((n, 2 * D_LRU), F32),
        ],
        scratch_shapes=[pltpu.VMEM((tm, D_MODEL), BF16)],
        compiler_params=_cparams(("parallel", "arbitrary")),
        name="inproj",
    )(x, g.reshape(1, D_MODEL), w_bf16)


def _conv_kernel(first_ref, last_ref, vm_ref, vp_ref, vn_ref,
                 w_ref, b_ref, lg_ref, lb_ref, o_ref, vext_ref, *, tm):
    i = pl.program_id(0)
    halo = CONV_HALO
    vext_ref[0:halo, :] = jnp.where(first_ref[i] == 1, 0.0, vp_ref[...])
    vext_ref[halo:halo + tm, :] = vm_ref[...]
    vext_ref[halo + tm:2 * halo + tm, :] = jnp.where(last_ref[i] == 1, 0.0, vn_ref[...])

    ch = CONV_CHUNK
    sub = SUBLANES_F32
    lead = halo - CONV_WIDTH // 2

    def body(c, carry):
        r0 = pl.multiple_of(c * ch, ch)
        for col in range(N_CONV_GROUPS):
            cs = slice(col * CONV_GROUP, (col + 1) * CONV_GROUP)
            acc = jnp.broadcast_to(b_ref[:, cs], (ch, CONV_GROUP))
            for s in range(sub):
                rows = ch + sub if s else ch
                part = None
                for k in range(CONV_WIDTH):
                    if (lead + k) % sub != s:
                        continue
                    term = w_ref[k:k + 1, cs] * vext_ref[pl.ds(r0 + (lead + k - s), rows), cs]
                    part = term if part is None else part + term
                acc = acc + part[s:s + ch, :]
            mu = jnp.mean(acc, axis=-1, keepdims=True)
            d = acc - mu
            var = jnp.mean(d * d, axis=-1, keepdims=True)
            y = d * lax.rsqrt(var + EPS) * lg_ref[:, cs] + lb_ref[:, cs]
            o_ref[pl.ds(r0, ch), cs] = (y * jax.nn.sigmoid(y)).astype(o_ref.dtype)
        return carry

    lax.fori_loop(0, tm // ch, body, 0)


def _conformer_conv(v, seqs, dw_w, dw_b, ln_g, ln_b):
    n = v.shape[0]
    tm, halo = TM_SEQ, CONV_HALO
    nt = n // tm
    hb = tm // halo
    nhb = n // halo
    first, last = _tile_flags(seqs, n, tm)

    def const(shape):
        return pl.BlockSpec(shape, lambda i, f, l: (0, 0))

    grid_spec = pltpu.PrefetchScalarGridSpec(
        num_scalar_prefetch=2,
        grid=(nt,),
        in_specs=[pl.BlockSpec((tm, D_CONV), lambda i, f, l: (i, 0)),
                  pl.BlockSpec((halo, D_CONV), lambda i, f, l: (jnp.maximum(i * hb - 1, 0), 0)),
                  pl.BlockSpec((halo, D_CONV), lambda i, f, l: (jnp.minimum((i + 1) * hb, nhb - 1), 0)),
                  const((CONV_WIDTH, D_CONV)), const((1, D_CONV)), const((1, D_CONV)), const((1, D_CONV))],
        out_specs=pl.BlockSpec((tm, D_CONV), lambda i, f, l: (i, 0)),
        scratch_shapes=[pltpu.VMEM((tm + 2 * halo, D_CONV), F32)],
    )
    return pl.pallas_call(
        functools.partial(_conv_kernel, tm=tm),
        grid_spec=grid_spec,
        out_shape=jax.ShapeDtypeStruct((n, D_CONV), BF16),
        compiler_params=_cparams(("parallel",)),
        name="conformer_conv",
    )(first, last, v, v, v,
      dw_w, dw_b.reshape(1, D_CONV), ln_g.reshape(1, D_CONV), ln_b.reshape(1, D_CONV))


N_BIAS_PAIRS = 2 * WIN_ROWS - 2


def _bias_pairs_kernel(rb_ref, o_ref):
    h = pl.program_id(0)
    n_dc = 2 * WIN_COLS - 1
    n_dr = 2 * WIN_ROWS - 1
    jj = lax.broadcasted_iota(jnp.int32, (GRID_W, 2 * GRID_W), 0)
    ll = lax.broadcasted_iota(jnp.int32, (GRID_W, 2 * GRID_W), 1)
    cc = ll & (GRID_W - 1)
    hi = ll >= GRID_W
    dc = cc - jj + (WIN_COLS - 1)
    cs = jnp.clip(jj - WIN_COLS // 2, 0, GRID_W - WIN_COLS)
    in_win = (cc >= cs) & (cc < cs + WIN_COLS)
    for a in range(N_BIAS_PAIRS):
        acc = jnp.zeros((GRID_W, 2 * GRID_W), F32)
        for d in range(n_dc):
            lo_v = rb_ref[(h * n_dr + a) * n_dc + d]
            hi_v = rb_ref[(h * n_dr + a + 1) * n_dc + d]
            acc = jnp.where(dc == d, jnp.where(hi, hi_v, lo_v), acc)
        o_ref[0, a] = jnp.where(in_win, acc, NEG_BIG)


def _bias_pairs(rel_bias):
    return pl.pallas_call(
        _bias_pairs_kernel,
        grid=(N_HEADS,),
        in_specs=[pl.BlockSpec(memory_space=pltpu.SMEM)],
        out_specs=pl.BlockSpec((1, N_BIAS_PAIRS, GRID_W, 2 * GRID_W), lambda h: (h, 0, 0, 0)),
        out_shape=jax.ShapeDtypeStruct((N_HEADS, N_BIAS_PAIRS, GRID_W, 2 * GRID_W), F32),
        compiler_params=_cparams(("parallel",)),
        name="na_bias_pairs",
    )(rel_bias.reshape(-1))


def _na_kernel(rs_ref, d0_ref, q_ref, k_ref, v_ref, p_ref, o_ref):
    blk = pl.program_id(1)
    scale = HEAD_DIM ** -0.5
    n_keys = WIN_ROWS * GRID_W
    starts, d0s = [], []
    for r in range(Q_ROWS):
        g = blk * Q_ROWS + r
        starts.append(pl.multiple_of(rs_ref[g] * GRID_W, GRID_W))
        d0s.append(d0_ref[g])
    scores = []
    for r in range(Q_ROWS):
        q = q_ref[r * GRID_W:(r + 1) * GRID_W, :]
        k = k_ref[pl.ds(starts[r], n_keys), :]
        scores.append(lax.dot_general(q, k, (((1,), (1,)), ((), ())), preferred_element_type=F32))
    probs, denoms = [], []
    for r in range(Q_ROWS):
        bias = jnp.concatenate([p_ref[0, d0s[r] + 2 * t] for t in range(WIN_ROWS // 2)], axis=1)
        s = scores[r] * scale + bias
        m = jnp.max(s, axis=-1, keepdims=True)
        p = jnp.exp(s - m)
        denoms.append(jnp.sum(p, axis=-1, keepdims=True))
        probs.append(p.astype(BF16))
    for r in range(Q_ROWS):
        v = v_ref[pl.ds(starts[r], n_keys), :]
        o = jnp.dot(probs[r], v, preferred_element_type=F32) / denoms[r]
        o_ref[r * GRID_W:(r + 1) * GRID_W, :] = o.astype(o_ref.dtype)


def _na_rows(seqs):
    rs, d0 = [], []
    for s, e in seqs:
        r0, r1 = s // GRID_W, e // GRID_W
        assert r1 - r0 >= WIN_ROWS
        for g in range(r0, r1):
            st = min(max(g - WIN_ROWS // 2, r0), r1 - WIN_ROWS)
            rs.append(st)
            d0.append(st - g + WIN_ROWS - 1)
    return jnp.asarray(np.array(rs, np.int32)), jnp.asarray(np.array(d0, np.int32))


def _neighbourhood_attention(qkv, seqs, bias_pairs):
    n = qkv.shape[0]
    tq = Q_ROWS * GRID_W
    assert n % tq == 0 and all((e - s) % tq == 0 for s, e in seqs)
    rs, d0 = _na_rows(seqs)
    grid_spec = pltpu.PrefetchScalarGridSpec(
        num_scalar_prefetch=2,
        grid=(N_HEADS, n // tq),
        in_specs=[
            pl.BlockSpec((tq, HEAD_DIM), lambda h, b, rs, d0: (b, h)),
            pl.BlockSpec((n, HEAD_DIM), lambda h, b, rs, d0: (0, N_HEADS + h)),
            pl.BlockSpec((n, HEAD_DIM), lambda h, b, rs, d0: (0, 2 * N_HEADS + h)),
            pl.BlockSpec((1, N_BIAS_PAIRS, GRID_W, 2 * GRID_W), lambda h, b, rs, d0: (h, 0, 0, 0)),
        ],
        out_specs=pl.BlockSpec((tq, HEAD_DIM), lambda h, b, rs, d0: (b, h)),
    )
    return pl.pallas_call(
        _na_kernel,
        grid_spec=grid_spec,
        out_shape=jax.ShapeDtypeStruct((n, D_ATT), BF16),
        compiler_params=_cparams(("parallel", "arbitrary")),
        name="neighbourhood_attention",
    )(rs, d0, qkv, qkv, qkv, bias_pairs)


def _log_sigmoid(x):
    return jnp.minimum(x, 0.0) - jnp.log1p(jnp.exp(-jnp.abs(x)))


def _chunk_scan(a, b, reverse):
    rows = a.shape[0]
    row = lax.broadcasted_iota(jnp.int32, a.shape, 0)
    d = 1
    while d < rows:
        if reverse:
            ok = row < rows - d
            a_sh = jnp.where(ok, pltpu.roll(a, rows - d, 0), 1.0)
            b_sh = jnp.where(ok, pltpu.roll(b, rows - d, 0), 0.0)
        else:
            ok = row >= d
            a_sh = jnp.where(ok, pltpu.roll(a, d, 0), 1.0)
            b_sh = jnp.where(ok, pltpu.roll(b, d, 0), 0.0)
        b = a * b_sh + b
        a = a * a_sh
        d *= 2
    return a, b


def _lru_kernel(first_ref, last_ref,
                xmf_ref, xpf_ref, xnf_ref, xmb_ref, xpb_ref, xnb_ref,
                cw_ref, cb_ref, w_ref, ba_ref, bx_ref, lam_ref,
                hf_ref, hb_ref,
                xext_ref, a_ref, b_ref, carry_ref, *, tm, nt):
    i = pl.program_id(0)
    halo = LRU_HALO
    lead = halo - LRU_CONV_WIDTH // 2

    @pl.when(i == 0)
    def _():
        carry_ref[...] = jnp.zeros_like(carry_ref)

    def gates(d, tile, xm_ref, xp_ref, xn_ref):
        xext_ref[0:halo, :] = jnp.where(first_ref[tile] == 1, 0.0, xp_ref[...])
        xext_ref[halo:halo + tm, :] = xm_ref[...]
        xext_ref[halo + tm:2 * halo + tm, :] = jnp.where(last_ref[tile] == 1, 0.0, xn_ref[...])
        xc = jnp.broadcast_to(cb_ref[...], (tm, D_LRU))
        for k in range(LRU_CONV_WIDTH):
            xc = xc + cw_ref[k:k + 1, :] * xext_ref[lead + k:lead + k + tm, :]
        log_s = _log_sigmoid(lam_ref[d])
        for nb in range(N_LRU_BLOCKS):
            sl = slice(nb * LRU_BLOCK, (nb + 1) * LRU_BLOCK)
            xb = xc[:, sl]
            z = jnp.dot(xb.astype(BF16), w_ref[d, nb], preferred_element_type=F32)
            r = jax.nn.sigmoid(z[:, :LRU_BLOCK] + ba_ref[d][:, sl])
            ig = jax.nn.sigmoid(z[:, LRU_BLOCK:] + bx_ref[d][:, sl])
            log_a = LRU_C * r * log_s[:, sl]
            a = jnp.exp(log_a)
            a_ref[d, :, sl] = a
            b_ref[d, :, sl] = jnp.sqrt(-jnp.tanh(log_a) * (1.0 + a * a)) * (ig * xb)

    gates(0, i, xmf_ref, xpf_ref, xnf_ref)
    gates(1, nt - 1 - i, xmb_ref, xpb_ref, xnb_ref)

    rows = SUBLANES_F32
    nch = tm // rows
    cf0 = jnp.where(first_ref[i] == 1, 0.0, carry_ref[0:1, :])
    cb0 = jnp.where(last_ref[nt - 1 - i] == 1, 0.0, carry_ref[1:2, :])

    def body(c, carry):
        cf, cb = carry
        rf = pl.multiple_of(c * rows, rows)
        rb = pl.multiple_of((nch - 1 - c) * rows, rows)
        af, hf = _chunk_scan(a_ref[0, pl.ds(rf, rows), :], b_ref[0, pl.ds(rf, rows), :], False)
        ab, hb = _chunk_scan(a_ref[1, pl.ds(rb, rows), :], b_ref[1, pl.ds(rb, rows), :], True)
        hf = hf + af * cf
        hb = hb + ab * cb
        hf_ref[pl.ds(rf, rows), :] = hf
        hb_ref[pl.ds(rb, rows), :] = hb
        return hf[rows - 1:rows, :], hb[0:1, :]

    cf, cb = lax.fori_loop(0, nch, body, (cf0, cb0))
    carry_ref[0:1, :] = cf
    carry_ref[1:2, :] = cb


def _rglru(u_lru, seqs, conv_w, conv_b, w_gates, ba, bx, lam):
    n = u_lru.shape[0]
    tm, halo = TM_SEQ, LRU_HALO
    nt = n // tm
    hb = tm // halo
    nhb = n // halo
    first, last = _tile_flags(seqs, n, tm)

    def tile_of(i, rev):
        return nt - 1 - i if rev else i

    def main(rev):
        return pl.BlockSpec((tm, D_LRU), lambda i, f, l: (tile_of(i, rev), 1))

    def prev(rev):
        return pl.BlockSpec((halo, D_LRU), lambda i, f, l: (jnp.maximum(tile_of(i, rev) * hb - 1, 0), 1))

    def nxt(rev):
        return pl.BlockSpec((halo, D_LRU), lambda i, f, l: (jnp.minimum((tile_of(i, rev) + 1) * hb, nhb - 1), 1))

    def const(shape):
        return pl.BlockSpec(shape, lambda i, f, l: (0,) * len(shape))

    grid_spec = pltpu.PrefetchScalarGridSpec(
        num_scalar_prefetch=2,
        grid=(nt,),
        in_specs=[main(False), prev(False), nxt(False), main(True), prev(True), nxt(True),
                  const((LRU_CONV_WIDTH, D_LRU)), const((1, D_LRU)),
                  const((2, N_LRU_BLOCKS, LRU_BLOCK, 2 * LRU_BLOCK)),
                  const((2, 1, D_LRU)), const((2, 1, D_LRU)), const((2, 1, D_LRU))],
        out_specs=[pl.BlockSpec((tm, D_LRU), lambda i, f, l: (i, 0)),
                   pl.BlockSpec((tm, D_LRU), lambda i, f, l: (nt - 1 - i, 0))],
        scratch_shapes=[pltpu.VMEM((tm + 2 * halo, D_LRU), F32),
                        pltpu.VMEM((2, tm, D_LRU), F32),
                        pltpu.VMEM((2, tm, D_LRU), F32),
                        pltpu.VMEM((SUBLANES_F32, D_LRU), F32)],
    )
    return pl.pallas_call(
        functools.partial(_lru_kernel, tm=tm, nt=nt),
        grid_spec=grid_spec,
        out_shape=[jax.ShapeDtypeStruct((n, D_LRU), F32), jax.ShapeDtypeStruct((n, D_LRU), F32)],
        compiler_params=_cparams(("arbitrary",)),
        name="rglru",
    )(first, last, u_lru, u_lru, u_lru, u_lru, u_lru, u_lru,
      conv_w, conv_b.reshape(1, D_LRU), w_gates,
      ba.reshape(2, 1, D_LRU), bx.reshape(2, 1, D_LRU), lam.reshape(2, 1, D_LRU))


def _outproj_kernel(x_ref, ya_ref, yb_ref, g_ref, hf_ref, hb_ref, w_ref, gn_ref, x1_ref, h2_ref):
    yc = (jax.nn.gelu(g_ref[...], approximate=True) * (hf_ref[...] + hb_ref[...])).astype(BF16)
    acc = jnp.dot(ya_ref[...], w_ref[0:D_CONV, :], preferred_element_type=F32)
    acc = acc + jnp.dot(yb_ref[...], w_ref[D_CONV:D_CONV + D_ATT, :], preferred_element_type=F32)
    acc = acc + jnp.dot(yc, w_ref[D_CONV + D_ATT:, :], preferred_element_type=F32)
    x1 = x_ref[...] + acc
    x1_ref[...] = x1
    h2_ref[...] = (x1 * _rms_scale(x1) * gn_ref[...]).astype(BF16)


def _outproj(x, ya, yb, u_lru, hf, hb, w_bf16, g_ffn, layer):
    n = x.shape[0]
    tm = TM_SEQ

    def rows(width, col=0):
        return pl.BlockSpec((tm, width), lambda i: (i, col))

    return pl.pallas_call(
        _outproj_kernel,
        grid=(n // tm,),
        in_specs=[rows(D_MODEL), rows(D_CONV), rows(D_ATT), rows(D_LRU, 0), rows(D_LRU), rows(D_LRU),
                  pl.BlockSpec((None, D_MODEL, D_MODEL), lambda i: (layer, 0, 0)),
                  pl.BlockSpec((1, D_MODEL), lambda i: (0, 0))],
        out_specs=[rows(D_MODEL), rows(D_MODEL)],
        out_shape=[jax.ShapeDtypeStruct((n, D_MODEL), F32), jax.ShapeDtypeStruct((n, D_MODEL), BF16)],
        compiler_params=_cparams(("parallel",)),
        name="outproj",
    )(x, ya, yb, u_lru, hf, hb, w_bf16, g_ffn.reshape(1, D_MODEL))


def _ffn_kernel(first_ref, last_ref, hm_ref, hp_ref, hn_ref, x1_ref,
                wg_ref, wv_ref, cwg_ref, cwv_ref, cbg_ref, cbv_ref, wd_ref, fg_ref,
                o_ref, hext_ref, ug0_ref, uv0_ref, ug1_ref, uv1_ref, act0_ref, act1_ref, acc_ref,
                *, tm, nt, nf, final_norm):
    t = pl.program_id(0)
    j = lax.rem(t, nf)
    i = jnp.minimum(lax.div(t, nf), nt - 1)
    halo = FFN_HALO

    @pl.when(t == 0)
    def _():
        for ref in (ug1_ref, uv1_ref, act0_ref, acc_ref):
            ref[...] = jnp.zeros_like(ref)

    @pl.when(j == 0)
    def _():
        hext_ref[0:halo, :] = jnp.where(first_ref[i] == 1, jnp.zeros_like(hp_ref), hp_ref[...])
        hext_ref[halo:halo + tm, :] = hm_ref[...]
        hext_ref[halo + tm:2 * halo + tm, :] = jnp.where(last_ref[i] == 1, jnp.zeros_like(hn_ref), hn_ref[...])

    def conv3(u_ref, cw_ref, cb_ref, r0, rows):
        base = halo + r0
        return (cw_ref[0:1, :] * u_ref[base - 1:base - 1 + rows, :]
                + cw_ref[1:2, :] * u_ref[base:base + rows, :]
                + cw_ref[2:3, :] * u_ref[base + 1:base + 1 + rows, :]
                + cb_ref[...])

    def step(ug_new, uv_new, ug_old, uv_old, act_new, act_old):
        rows = FFN_ACT_ROWS
        for r0 in range(0, tm, rows):
            gate = conv3(ug_old, cwg_ref, cbg_ref, r0, rows)
            val = conv3(uv_old, cwv_ref, cbv_ref, r0, rows)
            act_new[r0:r0 + rows, :] = (gate * jax.nn.sigmoid(gate) * val).astype(BF16)
        hext = hext_ref[...]
        ug_new[...] = jnp.dot(hext, wg_ref[...], preferred_element_type=F32)
        uv_new[...] = jnp.dot(hext, wv_ref[...], preferred_element_type=F32)
        acc_ref[...] += jnp.dot(act_old[...], wd_ref[...], preferred_element_type=F32)

    parity = lax.rem(t, 2)

    @pl.when(parity == 0)
    def _():
        step(ug0_ref, uv0_ref, ug1_ref, uv1_ref, act1_ref, act0_ref)

    @pl.when(parity == 1)
    def _():
        step(ug1_ref, uv1_ref, ug0_ref, uv0_ref, act0_ref, act1_ref)

    @pl.when((j == 1) & (t > 1))
    def _():
        x2 = x1_ref[...] + acc_ref[...]
        if final_norm:
            x2 = x2 * _rms_scale(x2) * fg_ref[...]
        o_ref[...] = x2

    @pl.when(j == 1)
    def _():
        acc_ref[...] = jnp.zeros_like(acc_ref)


def _ffn(h2, x1, seqs, w_up_bf16, conv_w, conv_b, w_down_bf16, final_g, layer, final_norm):
    n = h2.shape[0]
    tm, tf, halo = TM_SEQ, TF_FFN, FFN_HALO
    nt = n // tm
    nf = D_FF // tf
    hb = tm // halo
    nhb = n // halo
    first, last = _tile_flags(seqs, n, tm)
    cb = conv_b.reshape(1, 2 * D_FF)

    def tile(t):
        return jnp.minimum(lax.div(t, nf), nt - 1)

    def chunk(t):
        return lax.rem(t, nf)

    def tile_lag(t):
        return lax.div(jnp.maximum(t - 2, 0), nf)

    def chunk_lag(t, lag):
        return lax.rem(t + nf - lag, nf)

    grid_spec = pltpu.PrefetchScalarGridSpec(
        num_scalar_prefetch=2,
        grid=(nt * nf + 2,),
        in_specs=[
            pl.BlockSpec((tm, D_MODEL), lambda t, f, l: (tile(t), 0)),
            pl.BlockSpec((halo, D_MODEL), lambda t, f, l: (jnp.maximum(tile(t) * hb - 1, 0), 0)),
            pl.BlockSpec((halo, D_MODEL), lambda t, f, l: (jnp.minimum((tile(t) + 1) * hb, nhb - 1), 0)),
            pl.BlockSpec((tm, D_MODEL), lambda t, f, l: (tile_lag(t), 0)),
            pl.BlockSpec((None, D_MODEL, tf), lambda t, f, l: (layer, 0, chunk(t))),
            pl.BlockSpec((None, D_MODEL, tf), lambda t, f, l: (layer, 0, nf + chunk(t))),
            pl.BlockSpec((3, tf), lambda t, f, l: (0, chunk_lag(t, 1))),
            pl.BlockSpec((3, tf), lambda t, f, l: (0, nf + chunk_lag(t, 1))),
            pl.BlockSpec((1, tf), lambda t, f, l: (0, chunk_lag(t, 1))),
            pl.BlockSpec((1, tf), lambda t, f, l: (0, nf + chunk_lag(t, 1))),
            pl.BlockSpec((None, tf, D_MODEL), lambda t, f, l: (layer, chunk_lag(t, 2), 0)),
            pl.BlockSpec((1, D_MODEL), lambda t, f, l: (0, 0)),
        ],
        out_specs=pl.BlockSpec((tm, D_MODEL), lambda t, f, l: (tile_lag(t), 0)),
        scratch_shapes=[pltpu.VMEM((tm + 2 * halo, D_MODEL), BF16)]
        + [pltpu.VMEM((tm + 2 * halo, tf), F32)] * 4
        + [pltpu.VMEM((tm, tf), BF16)] * 2
        + [pltpu.VMEM((tm, D_MODEL), F32)],
    )
    return pl.pallas_call(
        functools.partial(_ffn_kernel, tm=tm, nt=nt, nf=nf, final_norm=final_norm),
        grid_spec=grid_spec,
        out_shape=jax.ShapeDtypeStruct((n, D_MODEL), F32),
        compiler_params=_cparams(("arbitrary",)),
        name="ffn_final" if final_norm else "ffn",
    )(first, last, h2, h2, h2, x1, w_up_bf16, w_up_bf16, conv_w, conv_w, cb, cb,
      w_down_bf16, final_g.reshape(1, D_MODEL))


def _trunk(x, seqs, p):
    depth = p["w_in"].shape[0]
    for l in range(depth):
        v_conv, qkv, u_lru = _inproj(x, p["norm_mix_g"][l], p["w_in"], l)
        ya = _conformer_conv(v_conv, seqs, p["conv_dw_w"][l], p["conv_dw_b"][l], p["conv_ln_g"][l], p["conv_ln_b"][l])
        yb = _neighbourhood_attention(qkv, seqs, p["bias_pairs"][l])
        hf, hb = _rglru(u_lru, seqs, p["lru_conv_w"][l], p["lru_conv_b"][l], p["lru_w_gates"][l],
                        p["lru_ba"][l], p["lru_bx"][l], p["lru_lambda"][l])
        x1, h2 = _outproj(x, ya, yb, u_lru, hf, hb, p["w_out"], p["norm_ffn_g"][l], l)
        x = _ffn(h2, x1, seqs, p["ffn_w_up"], p["ffn_conv_w"][l], p["ffn_conv_b"][l], p["ffn_w_down"],
                 p["final_norm_g"], l, final_norm=(l == depth - 1))
    return x


def kernel(x_prompt, x_sample, norm_mix_g, w_in, conv_dw_w, conv_dw_b, conv_ln_g, conv_ln_b, na_rel_bias, lru_conv_w, lru_conv_b, lru_wa, lru_ba, lru_wx, lru_bx, lru_lambda, w_out, norm_ffn_g, ffn_w_up, ffn_conv_w, ffn_conv_b, ffn_w_down, final_norm_g):
    depth = w_in.shape[0]
    p = dict(
        norm_mix_g=norm_mix_g, conv_dw_w=conv_dw_w, conv_dw_b=conv_dw_b, conv_ln_g=conv_ln_g, conv_ln_b=conv_ln_b,
        lru_conv_w=lru_conv_w, lru_conv_b=lru_conv_b, lru_ba=lru_ba, lru_bx=lru_bx, lru_lambda=lru_lambda,
        norm_ffn_g=norm_ffn_g, ffn_conv_w=ffn_conv_w, ffn_conv_b=ffn_conv_b, final_norm_g=final_norm_g,
        w_in=w_in.astype(BF16), w_out=w_out.astype(BF16),
        ffn_w_up=ffn_w_up.astype(BF16), ffn_w_down=ffn_w_down.astype(BF16),
        lru_w_gates=jnp.concatenate([lru_wa, lru_wx], axis=-1).astype(BF16),
        bias_pairs=[_bias_pairs(na_rel_bias[l]) for l in range(depth)],
    )
    outs = []
    for xb in (x_prompt, x_sample):
        b, t, _ = xb.shape
        seqs = tuple((s * t, (s + 1) * t) for s in range(b))
        outs.append(_trunk(xb.reshape(b * t, D_MODEL), seqs, p).reshape(xb.shape))
    return tuple(outs)
```

```python
import functools

import numpy as np
import jax
import jax.numpy as jnp
from jax import lax
from jax.experimental import pallas as pl
from jax.experimental.pallas import tpu as pltpu

F32 = jnp.float32
BF16 = jnp.bfloat16

D_MODEL = 2048
GRID_W = 64
D_CONV = 512
D_ATT = 1024
D_LRU = 512
N_CONV_GROUPS = 4
CONV_GROUP = 128
CONV_WIDTH = 31
HEAD_DIM = 128
N_HEADS = 8
WIN_ROWS = 8
WIN_COLS = 16
N_LRU_BLOCKS = 4
LRU_BLOCK = 128
LRU_CONV_WIDTH = 4
LRU_C = 8.0
D_FF = 5632
D_IN = 5120
EPS = 1e-6
NEG_BIG = -1e30

LANES = 128
SUBLANES_F32 = 8
SUBLANES_BF16 = 16
VMEM_LIMIT_BYTES = 56 * 1024 * 1024

TM_PROJ = 1024
TN_PROJ = 1024
TM_SEQ = 512
TF_FFN = 512
CONV_HALO = 16
LRU_HALO = 8
FFN_HALO = 16
FFN_ACT_ROWS = 32
CONV_CHUNK = 128
Q_ROWS = 8


def _cparams(semantics):
    return pltpu.CompilerParams(dimension_semantics=semantics,
                                vmem_limit_bytes=VMEM_LIMIT_BYTES)


def _tile_flags(seqs, n_tokens, tm):
    starts = {s for s, _ in seqs}
    ends = {e for _, e in seqs}
    nt = n_tokens // tm
    first = np.array([1 if i * tm in starts else 0 for i in range(nt)], np.int32)
    last = np.array([1 if (i + 1) * tm in ends else 0 for i in range(nt)], np.int32)
    return jnp.asarray(first), jnp.asarray(last)


def _rms_scale(x):
    return lax.rsqrt(jnp.mean(x * x, axis=-1, keepdims=True) + EPS)


def _inproj_kernel(x_ref, g_ref, w_ref, ov_ref, oqkv_ref, ol_ref, h_ref):
    j = pl.program_id(1)

    @pl.when(j == 0)
    def _():
        x = x_ref[...]
        h_ref[...] = (x * _rms_scale(x) * g_ref[...]).astype(BF16)

    r = jnp.dot(h_ref[...], w_ref[...], preferred_element_type=F32)

    @pl.when(j == 0)
    def _():
        ov_ref[...] = r[:, :D_CONV] * jax.nn.sigmoid(r[:, D_CONV:])

    @pl.when((j >= 1) & (j <= 3))
    def _():
        oqkv_ref[...] = r.astype(BF16)

    @pl.when(j == 4)
    def _():
        ol_ref[...] = r


def _inproj(x, g, w_bf16, layer):
    n = x.shape[0]
    tm, tn = TM_PROJ, TN_PROJ
    assert n % tm == 0 and D_IN == 5 * tn and 2 * D_CONV == tn and D_ATT == tn
    return pl.pallas_call(
        _inproj_kernel,
        grid=(n // tm, D_IN // tn),
        in_specs=[
            pl.BlockSpec((tm, D_MODEL), lambda i, j: (i, 0)),
            pl.BlockSpec((1, D_MODEL), lambda i, j: (0, 0)),
            pl.BlockSpec((None, D_MODEL, tn), lambda i, j: (layer, 0, j)),
        ],
        out_specs=[
            pl.BlockSpec((tm, D_CONV), lambda i, j: (i, 0)),
            pl.BlockSpec((tm, tn), lambda i, j: (i, jnp.clip(j - 1, 0, 2))),
            pl.BlockSpec((tm, tn), lambda i, j: (i, 0)),
        ],
        out_shape=[
            jax.ShapeDtypeStruct((n, D_CONV), F32),
            jax.ShapeDtypeStruct((n, 3 * D_ATT), BF16),
            jax.ShapeDtypeStruct((n, 2 * D_LRU), F32),
        ],
        scratch_shapes=[pltpu.VMEM((tm, D_MODEL), BF16)],
        compiler_params=_cparams(("parallel", "arbitrary")),
        name="inproj",
    )(x, g.reshape(1, D_MODEL), w_bf16)


def _conv_kernel(first_ref, last_ref, vm_ref, vp_ref, vn_ref,
                 w_ref, b_ref, lg_ref, lb_ref, o_ref, vext_ref, *, tm):
    i = pl.program_id(0)
    halo = CONV_HALO
    vext_ref[0:halo, :] = jnp.where(first_ref[i] == 1, 0.0, vp_ref[...])
    vext_ref[halo:halo + tm, :] = vm_ref[...]
    vext_ref[halo + tm:2 * halo + tm, :] = jnp.where(last_ref[i] == 1, 0.0, vn_ref[...])

    ch = CONV_CHUNK
    sub = SUBLANES_F32
    lead = halo - CONV_WIDTH // 2

    def body(c, carry):
        r0 = pl.multiple_of(c * ch, ch)
        for col in range(N_CONV_GROUPS):
            cs = slice(col * CONV_GROUP, (col + 1) * CONV_GROUP)
            acc = jnp.broadcast_to(b_ref[:, cs], (ch, CONV_GROUP))
            for s in range(sub):
                rows = ch + sub if s else ch
                part = None
                for k in range(CONV_WIDTH):
                    if (lead + k) % sub != s:
                        continue
                    term = w_ref[k:k + 1, cs] * vext_ref[pl.ds(r0 + (lead + k - s), rows), cs]
                    part = term if part is None else part + term
                acc = acc + part[s:s + ch, :]
            mu = jnp.mean(acc, axis=-1, keepdims=True)
            d = acc - mu
            var = jnp.mean(d * d, axis=-1, keepdims=True)
            y = d * lax.rsqrt(var + EPS) * lg_ref[:, cs] + lb_ref[:, cs]
            o_ref[pl.ds(r0, ch), cs] = (y * jax.nn.sigmoid(y)).astype(o_ref.dtype)
        return carry

    lax.fori_loop(0, tm // ch, body, 0)


def _conformer_conv(v, seqs, dw_w, dw_b, ln_g, ln_b):
    n = v.shape[0]
    tm, halo = TM_SEQ, CONV_HALO
    nt = n // tm
    hb = tm // halo
    nhb = n // halo
    first, last = _tile_flags(seqs, n, tm)

    def const(shape):
        return pl.BlockSpec(shape, lambda i, f, l: (0, 0))

    grid_spec = pltpu.PrefetchScalarGridSpec(
        num_scalar_prefetch=2,
        grid=(nt,),
        in_specs=[pl.BlockSpec((tm, D_CONV), lambda i, f, l: (i, 0)),
                  pl.BlockSpec((halo, D_CONV), lambda i, f, l: (jnp.maximum(i * hb - 1, 0), 0)),
                  pl.BlockSpec((halo, D_CONV), lambda i, f, l: (jnp.minimum((i + 1) * hb, nhb - 1), 0)),
                  const((CONV_WIDTH, D_CONV)), const((1, D_CONV)), const((1, D_CONV)), const((1, D_CONV))],
        out_specs=pl.BlockSpec((tm, D_CONV), lambda i, f, l: (i, 0)),
        scratch_shapes=[pltpu.VMEM((tm + 2 * halo, D_CONV), F32)],
    )
    return pl.pallas_call(
        functools.partial(_conv_kernel, tm=tm),
        grid_spec=grid_spec,
        out_shape=jax.ShapeDtypeStruct((n, D_CONV), BF16),
        compiler_params=_cparams(("parallel",)),
        name="conformer_conv",
    )(first, last, v, v, v,
      dw_w, dw_b.reshape(1, D_CONV), ln_g.reshape(1, D_CONV), ln_b.reshape(1, D_CONV))


N_BIAS_PAIRS = 2 * WIN_ROWS - 2


def _bias_pairs_kernel(rb_ref, o_ref):
    h = pl.program_id(0)
    n_dc = 2 * WIN_COLS - 1
    n_dr = 2 * WIN_ROWS - 1
    jj = lax.broadcasted_iota(jnp.int32, (GRID_W, 2 * GRID_W), 0)
    ll = lax.broadcasted_iota(jnp.int32, (GRID_W, 2 * GRID_W), 1)
    cc = ll & (GRID_W - 1)
    hi = ll >= GRID_W
    dc = cc - jj + (WIN_COLS - 1)
    cs = jnp.clip(jj - WIN_COLS // 2, 0, GRID_W - WIN_COLS)
    in_win = (cc >= cs) & (cc < cs + WIN_COLS)
    for a in range(N_BIAS_PAIRS):
        acc = jnp.zeros((GRID_W, 2 * GRID_W), F32)
        for d in range(n_dc):
            lo_v = rb_ref[(h * n_dr + a) * n_dc + d]
            hi_v = rb_ref[(h * n_dr + a + 1) * n_dc + d]
            acc = jnp.where(dc == d, jnp.where(hi, hi_v, lo_v), acc)
        o_ref[0, a] = jnp.where(in_win, acc, NEG_BIG)


def _bias_pairs(rel_bias):
    return pl.pallas_call(
        _bias_pairs_kernel,
        grid=(N_HEADS,),
        in_specs=[pl.BlockSpec(memory_space=pltpu.SMEM)],
        out_specs=pl.BlockSpec((1, N_BIAS_PAIRS, GRID_W, 2 * GRID_W), lambda h: (h, 0, 0, 0)),
        out_shape=jax.ShapeDtypeStruct((N_HEADS, N_BIAS_PAIRS, GRID_W, 2 * GRID_W), F32),
        compiler_params=_cparams(("parallel",)),
        name="na_bias_pairs",
    )(rel_bias.reshape(-1))


def _na_kernel(rs_ref, d0_ref, q_ref, k_ref, v_ref, p_ref, o_ref):
    blk = pl.program_id(1)
    scale = HEAD_DIM ** -0.5
    n_keys = WIN_ROWS * GRID_W
    starts, d0s = [], []
    for r in range(Q_ROWS):
        g = blk * Q_ROWS + r
        starts.append(pl.multiple_of(rs_ref[g] * GRID_W, GRID_W))
        d0s.append(d0_ref[g])
    scores = []
    for r in range(Q_ROWS):
        q = q_ref[r * GRID_W:(r + 1) * GRID_W, :]
        k = k_ref[pl.ds(starts[r], n_keys), :]
        scores.append(lax.dot_general(q, k, (((1,), (1,)), ((), ())), preferred_element_type=F32))
    probs, denoms = [], []
    for r in range(Q_ROWS):
        bias = jnp.concatenate([p_ref[0, d0s[r] + 2 * t] for t in range(WIN_ROWS // 2)], axis=1)
        s = scores[r] * scale + bias
        m = jnp.max(s, axis=-1, keepdims=True)
        p = jnp.exp(s - m)
        denoms.append(jnp.sum(p, axis=-1, keepdims=True))
        probs.append(p.astype(BF16))
    for r in range(Q_ROWS):
        v = v_ref[pl.ds(starts[r], n_keys), :]
        o = jnp.dot(probs[r], v, preferred_element_type=F32) / denoms[r]
        o_ref[r * GRID_W:(r + 1) * GRID_W, :] = o.astype(o_ref.dtype)


def _na_rows(seqs):
    rs, d0 = [], []
    for s, e in seqs:
        r0, r1 = s // GRID_W, e // GRID_W
        assert r1 - r0 >= WIN_ROWS
        for g in range(r0, r1):
            st = min(max(g - WIN_ROWS // 2, r0), r1 - WIN_ROWS)
            rs.append(st)
            d0.append(st - g + WIN_ROWS - 1)
    return jnp.asarray(np.array(rs, np.int32)), jnp.asarray(np.array(d0, np.int32))


def _neighbourhood_attention(qkv, seqs, bias_pairs):
    n = qkv.shape[0]
    tq = Q_ROWS * GRID_W
    assert n % tq == 0 and all((e - s) % tq == 0 for s, e in seqs)
    rs, d0 = _na_rows(seqs)
    grid_spec = pltpu.PrefetchScalarGridSpec(
        num_scalar_prefetch=2,
        grid=(N_HEADS, n // tq),
        in_specs=[
            pl.BlockSpec((tq, HEAD_DIM), lambda h, b, rs, d0: (b, h)),
            pl.BlockSpec((n, HEAD_DIM), lambda h, b, rs, d0: (0, N_HEADS + h)),
            pl.BlockSpec((n, HEAD_DIM), lambda h, b, rs, d0: (0, 2 * N_HEADS + h)),
            pl.BlockSpec((1, N_BIAS_PAIRS, GRID_W, 2 * GRID_W), lambda h, b, rs, d0: (h, 0, 0, 0)),
        ],
        out_specs=pl.BlockSpec((tq, HEAD_DIM), lambda h, b, rs, d0: (b, h)),
    )
    return pl.pallas_call(
        _na_kernel,
        grid_spec=grid_spec,
        out_shape=jax.ShapeDtypeStruct((n, D_ATT), BF16),
        compiler_params=_cparams(("parallel", "arbitrary")),
        name="neighbourhood_attention",
    )(rs, d0, qkv, qkv, qkv, bias_pairs)


def _log_sigmoid(x):
    return jnp.minimum(x, 0.0) - jnp.log1p(jnp.exp(-jnp.abs(x)))


def _chunk_scan(a, b, reverse):
    rows = a.shape[0]
    row = lax.broadcasted_iota(jnp.int32, a.shape, 0)
    d = 1
    while d < rows:
        if reverse:
            ok = row < rows - d
            a_sh = jnp.where(ok, pltpu.roll(a, rows - d, 0), 1.0)
            b_sh = jnp.where(ok, pltpu.roll(b, rows - d, 0), 0.0)
        else:
            ok = row >= d
            a_sh = jnp.where(ok, pltpu.roll(a, d, 0), 1.0)
            b_sh = jnp.where(ok, pltpu.roll(b, d, 0), 0.0)
        b = a * b_sh + b
        a = a * a_sh
        d *= 2
    return a, b


def _lru_kernel(first_ref, last_ref,
                xmf_ref, xpf_ref, xnf_ref, xmb_ref, xpb_ref, xnb_ref,
                cw_ref, cb_ref, w_ref, ba_ref, bx_ref, lam_ref,
                hf_ref, hb_ref,
                xext_ref, a_ref, b_ref, carry_ref, *, tm, nt):
    i = pl.program_id(0)
    halo = LRU_HALO
    lead = halo - LRU_CONV_WIDTH // 2

    @pl.when(i == 0)
    def _():
        carry_ref[...] = jnp.zeros_like(carry_ref)

    def gates(d, tile, xm_ref, xp_ref, xn_ref):
        xext_ref[0:halo, :] = jnp.where(first_ref[tile] == 1, 0.0, xp_ref[...])
        xext_ref[halo:halo + tm, :] = xm_ref[...]
        xext_ref[halo + tm:2 * halo + tm, :] = jnp.where(last_ref[tile] == 1, 0.0, xn_ref[...])
        xc = jnp.broadcast_to(cb_ref[...], (tm, D_LRU))
        for k in range(LRU_CONV_WIDTH):
            xc = xc + cw_ref[k:k + 1, :] * xext_ref[lead + k:lead + k + tm, :]
        log_s = _log_sigmoid(lam_ref[d])
        for nb in range(N_LRU_BLOCKS):
            sl = slice(nb * LRU_BLOCK, (nb + 1) * LRU_BLOCK)
            xb = xc[:, sl]
            z = jnp.dot(xb.astype(BF16), w_ref[d, nb], preferred_element_type=F32)
            r = jax.nn.sigmoid(z[:, :LRU_BLOCK] + ba_ref[d][:, sl])
            ig = jax.nn.sigmoid(z[:, LRU_BLOCK:] + bx_ref[d][:, sl])
            log_a = LRU_C * r * log_s[:, sl]
            a = jnp.exp(log_a)
            a_ref[d, :, sl] = a
            b2 = -jnp.tanh(log_a) * (1.0 + a * a)
            b_ref[d, :, sl] = jnp.where(b2 > 0.0, b2 * lax.rsqrt(b2), 0.0) * (ig * xb)

    gates(0, i, xmf_ref, xpf_ref, xnf_ref)
    gates(1, nt - 1 - i, xmb_ref, xpb_ref, xnb_ref)

    rows = SUBLANES_F32
    nch = tm // rows
    cf0 = jnp.where(first_ref[i] == 1, 0.0, carry_ref[0:1, :])
    cb0 = jnp.where(last_ref[nt - 1 - i] == 1, 0.0, carry_ref[1:2, :])

    def body(c, carry):
        cf, cb = carry
        rf = pl.multiple_of(c * rows, rows)
        rb = pl.multiple_of((nch - 1 - c) * rows, rows)
        af, hf = _chunk_scan(a_ref[0, pl.ds(rf, rows), :], b_ref[0, pl.ds(rf, rows), :], False)
        ab, hb = _chunk_scan(a_ref[1, pl.ds(rb, rows), :], b_ref[1, pl.ds(rb, rows), :], True)
        hf = hf + af * cf
        hb = hb + ab * cb
        hf_ref[pl.ds(rf, rows), :] = hf
        hb_ref[pl.ds(rb, rows), :] = hb
        return hf[rows - 1:rows, :], hb[0:1, :]

    cf, cb = lax.fori_loop(0, nch, body, (cf0, cb0))
    carry_ref[0:1, :] = cf
    carry_ref[1:2, :] = cb


def _rglru(u_lru, seqs, conv_w, conv_b, w_gates, ba, bx, lam):
    n = u_lru.shape[0]
    tm, halo = TM_SEQ, LRU_HALO
    nt = n // tm
    hb = tm // halo
    nhb = n // halo
    first, last = _tile_flags(seqs, n, tm)

    def tile_of(i, rev):
        return nt - 1 - i if rev else i

    def main(rev):
        return pl.BlockSpec((tm, D_LRU), lambda i, f, l: (tile_of(i, rev), 1))

    def prev(rev):
        return pl.BlockSpec((halo, D_LRU), lambda i, f, l: (jnp.maximum(tile_of(i, rev) * hb - 1, 0), 1))

    def nxt(rev):
        return pl.BlockSpec((halo, D_LRU), lambda i, f, l: (jnp.minimum((tile_of(i, rev) + 1) * hb, nhb - 1), 1))

    def const(shape):
        return pl.BlockSpec(shape, lambda i, f, l: (0,) * len(shape))

    grid_spec = pltpu.PrefetchScalarGridSpec(
        num_scalar_prefetch=2,
        grid=(nt,),
        in_specs=[main(False), prev(False), nxt(False), main(True), prev(True), nxt(True),
                  const((LRU_CONV_WIDTH, D_LRU)), const((1, D_LRU)),
                  const((2, N_LRU_BLOCKS, LRU_BLOCK, 2 * LRU_BLOCK)),
                  const((2, 1, D_LRU)), const((2, 1, D_LRU)), const((2, 1, D_LRU))],
        out_specs=[pl.BlockSpec((tm, D_LRU), lambda i, f, l: (i, 0)),
                   pl.BlockSpec((tm, D_LRU), lambda i, f, l: (nt - 1 - i, 0))],
        scratch_shapes=[pltpu.VMEM((tm + 2 * halo, D_LRU), F32),
                        pltpu.VMEM((2, tm, D_LRU), F32),
                        pltpu.VMEM((2, tm, D_LRU), F32),
                        pltpu.VMEM((SUBLANES_F32, D_LRU), F32)],
    )
    return pl.pallas_call(
        functools.partial(_lru_kernel, tm=tm, nt=nt),
        grid_spec=grid_spec,
        out_shape=[jax.ShapeDtypeStruct((n, D_LRU), F32), jax.ShapeDtypeStruct((n, D_LRU), F32)],
        compiler_params=_cparams(("arbitrary",)),
        name="rglru",
    )(first, last, u_lru, u_lru, u_lru, u_lru, u_lru, u_lru,
      conv_w, conv_b.reshape(1, D_LRU), w_gates,
      ba.reshape(2, 1, D_LRU), bx.reshape(2, 1, D_LRU), lam.reshape(2, 1, D_LRU))


def _outproj_kernel(x_ref, ya_ref, yb_ref, g_ref, hf_ref, hb_ref, w_ref, gn_ref, x1_ref, h2_ref):
    yc = (jax.nn.gelu(g_ref[...], approximate=True) * (hf_ref[...] + hb_ref[...])).astype(BF16)
    acc = jnp.dot(ya_ref[...], w_ref[0:D_CONV, :], preferred_element_type=F32)
    acc = acc + jnp.dot(yb_ref[...], w_ref[D_CONV:D_CONV + D_ATT, :], preferred_element_type=F32)
    acc = acc + jnp.dot(yc, w_ref[D_CONV + D_ATT:, :], preferred_element_type=F32)
    x1 = x_ref[...] + acc
    x1_ref[...] = x1
    h2_ref[...] = (x1 * _rms_scale(x1) * gn_ref[...]).astype(BF16)


def _outproj(x, ya, yb, u_lru, hf, hb, w_bf16, g_ffn, layer):
    n = x.shape[0]
    tm = TM_SEQ

    def rows(width, col=0):
        return pl.BlockSpec((tm, width), lambda i: (i, col))

    return pl.pallas_call(
        _outproj_kernel,
        grid=(n // tm,),
        in_specs=[rows(D_MODEL), rows(D_CONV), rows(D_ATT), rows(D_LRU, 0), rows(D_LRU), rows(D_LRU),
                  pl.BlockSpec((None, D_MODEL, D_MODEL), lambda i: (layer, 0, 0)),
                  pl.BlockSpec((1, D_MODEL), lambda i: (0, 0))],
        out_specs=[rows(D_MODEL), rows(D_MODEL)],
        out_shape=[jax.ShapeDtypeStruct((n, D_MODEL), F32), jax.ShapeDtypeStruct((n, D_MODEL), BF16)],
        compiler_params=_cparams(("parallel",)),
        name="outproj",
    )(x, ya, yb, u_lru, hf, hb, w_bf16, g_ffn.reshape(1, D_MODEL))


def _ffn_kernel(first_ref, last_ref, hm_ref, hp_ref, hn_ref, x1_ref,
                wg_ref, wv_ref, cwg_ref, cwv_ref, cbg_ref, cbv_ref, wd_ref, fg_ref,
                o_ref, hext_ref, *, tm, final_norm):
    i = pl.program_id(0)
    j = pl.program_id(1)
    halo = FFN_HALO
    rows_ext = tm + halo

    @pl.when(j == 0)
    def _():
        row = lax.broadcasted_iota(jnp.int32, hp_ref.shape, 0)
        before = jnp.where(first_ref[i] == 1, jnp.zeros_like(hp_ref), hp_ref[...])
        after = jnp.where(last_ref[i] == 1, jnp.zeros_like(hn_ref), hn_ref[...])
        hext_ref[0:halo, :] = jnp.where(row == 0, after.astype(F32), before.astype(F32)).astype(BF16)
        hext_ref[halo:rows_ext, :] = hm_ref[...]
        o_ref[...] = x1_ref[...]

    hext = hext_ref[...]

    def conv3(w_ref, cw_ref, cb_ref):
        u = jnp.dot(hext, w_ref[...], preferred_element_type=F32)
        prev = pltpu.roll(u, 1, 0)[halo:, :]
        nxt = pltpu.roll(u, rows_ext - 1, 0)[halo:, :]
        return (cw_ref[0:1, :] * prev + cw_ref[1:2, :] * u[halo:, :] + cw_ref[2:3, :] * nxt
                + cb_ref[...])

    gate = conv3(wg_ref, cwg_ref, cbg_ref)
    val = conv3(wv_ref, cwv_ref, cbv_ref)
    act = (gate * jax.nn.sigmoid(gate) * val).astype(BF16)
    o_ref[...] += jnp.dot(act, wd_ref[...], preferred_element_type=F32)

    if final_norm:
        @pl.when(j == pl.num_programs(1) - 1)
        def _():
            x2 = o_ref[...]
            o_ref[...] = x2 * _rms_scale(x2) * fg_ref[...]


def _ffn(h2, x1, seqs, w_up_bf16, conv_w, conv_b, w_down_bf16, final_g, layer, final_norm):
    n = h2.shape[0]
    tm, tf, halo = TM_SEQ, TF_FFN, FFN_HALO
    nt = n // tm
    nf = D_FF // tf
    hb = tm // halo
    nhb = n // halo
    first, last = _tile_flags(seqs, n, tm)
    cb = conv_b.reshape(1, 2 * D_FF)
    grid_spec = pltpu.PrefetchScalarGridSpec(
        num_scalar_prefetch=2,
        grid=(nt, nf),
        in_specs=[
            pl.BlockSpec((tm, D_MODEL), lambda i, j, f, l: (i, 0)),
            pl.BlockSpec((halo, D_MODEL), lambda i, j, f, l: (jnp.maximum(i * hb - 1, 0), 0)),
            pl.BlockSpec((halo, D_MODEL), lambda i, j, f, l: (jnp.minimum((i + 1) * hb, nhb - 1), 0)),
            pl.BlockSpec((tm, D_MODEL), lambda i, j, f, l: (i, 0)),
            pl.BlockSpec((None, D_MODEL, tf), lambda i, j, f, l: (layer, 0, j)),
            pl.BlockSpec((None, D_MODEL, tf), lambda i, j, f, l: (layer, 0, nf + j)),
            pl.BlockSpec((3, tf), lambda i, j, f, l: (0, j)),
            pl.BlockSpec((3, tf), lambda i, j, f, l: (0, nf + j)),
            pl.BlockSpec((1, tf), lambda i, j, f, l: (0, j)),
            pl.BlockSpec((1, tf), lambda i, j, f, l: (0, nf + j)),
            pl.BlockSpec((None, tf, D_MODEL), lambda i, j, f, l: (layer, j, 0)),
            pl.BlockSpec((1, D_MODEL), lambda i, j, f, l: (0, 0)),
        ],
        out_specs=pl.BlockSpec((tm, D_MODEL), lambda i, j, f, l: (i, 0)),
        scratch_shapes=[pltpu.VMEM((tm + halo, D_MODEL), BF16)],
    )
    return pl.pallas_call(
        functools.partial(_ffn_kernel, tm=tm, final_norm=final_norm),
        grid_spec=grid_spec,
        out_shape=jax.ShapeDtypeStruct((n, D_MODEL), F32),
        compiler_params=_cparams(("parallel", "arbitrary")),
        name="ffn_final" if final_norm else "ffn",
    )(first, last, h2, h2, h2, x1, w_up_bf16, w_up_bf16, conv_w, conv_w, cb, cb,
      w_down_bf16, final_g.reshape(1, D_MODEL))


def _trunk(x, seqs, p):
    depth = p["w_in"].shape[0]
    for l in range(depth):
        v_conv, qkv, u_lru = _inproj(x, p["norm_mix_g"][l], p["w_in"], l)
        ya = _conformer_conv(v_conv, seqs, p["conv_dw_w"][l], p["conv_dw_b"][l], p["conv_ln_g"][l], p["conv_ln_b"][l])
        yb = _neighbourhood_attention(qkv, seqs, p["bias_pairs"][l])
        hf, hb = _rglru(u_lru, seqs, p["lru_conv_w"][l], p["lru_conv_b"][l], p["lru_w_gates"][l],
                        p["lru_ba"][l], p["lru_bx"][l], p["lru_lambda"][l])
        x1, h2 = _outproj(x, ya, yb, u_lru, hf, hb, p["w_out"], p["norm_ffn_g"][l], l)
        x = _ffn(h2, x1, seqs, p["ffn_w_up"], p["ffn_conv_w"][l], p["ffn_conv_b"][l], p["ffn_w_down"],
                 p["final_norm_g"], l, final_norm=(l == depth - 1))
    return x


def kernel(x_prompt, x_sample, norm_mix_g, w_in, conv_dw_w, conv_dw_b, conv_ln_g, conv_ln_b, na_rel_bias, lru_conv_w, lru_conv_b, lru_wa, lru_ba, lru_wx, lru_bx, lru_lambda, w_out, norm_ffn_g, ffn_w_up, ffn_conv_w, ffn_conv_b, ffn_w_down, final_norm_g):
    depth = w_in.shape[0]
    p = dict(
        norm_mix_g=norm_mix_g, conv_dw_w=conv_dw_w, conv_dw_b=conv_dw_b, conv_ln_g=conv_ln_g, conv_ln_b=conv_ln_b,
        lru_conv_w=lru_conv_w, lru_conv_b=lru_conv_b, lru_ba=lru_ba, lru_bx=lru_bx, lru_lambda=lru_lambda,
        norm_ffn_g=norm_ffn_g, ffn_conv_w=ffn_conv_w, ffn_conv_b=ffn_conv_b, final_norm_g=final_norm_g,
        w_in=w_in.astype(BF16), w_out=w_out.astype(BF16),
        ffn_w_up=ffn_w_up.astype(BF16), ffn_w_down=ffn_w_down.astype(BF16),
        lru_w_gates=jnp.concatenate([lru_wa, lru_wx], axis=-1).astype(BF16),
        bias_pairs=[_bias_pairs(na_rel_bias[l]) for l in range(depth)],
    )
    outs = []
    for xb in (x_prompt, x_sample):
        b, t, _ = xb.shape
        seqs = tuple((s * t, (s + 1) * t) for s in range(b))
        outs.append(_trunk(xb.reshape(b * t, D_MODEL), seqs, p).reshape(xb.shape))
    return tuple(outs)
```

```python
import functools

import numpy as np
import jax
import jax.numpy as jnp
from jax import lax
from jax.experimental import pallas as pl
from jax.experimental.pallas import tpu as pltpu

F32 = jnp.float32
BF16 = jnp.bfloat16

D_MODEL = 2048
GRID_W = 64
D_CONV = 512
D_ATT = 1024
D_LRU = 512
N_CONV_GROUPS = 4
CONV_GROUP = 128
CONV_WIDTH = 31
HEAD_DIM = 128
N_HEADS = 8
WIN_ROWS = 8
WIN_COLS = 16
N_LRU_BLOCKS = 4
LRU_BLOCK = 128
LRU_CONV_WIDTH = 4
LRU_C = 8.0
D_FF = 5632
D_IN = 5120
EPS = 1e-6
NEG_BIG = -1e30

SUBLANES_F32 = 8
SUBLANES_BF16 = 16
VMEM_LIMIT_BYTES = 56 * 1024 * 1024

TM_PROJ = 1024
TN_PROJ = 1024
TM_SEQ = 512
TF_FFN = 512
CONV_HALO = 16
LRU_HALO = 8
FFN_HALO = SUBLANES_BF16
CONV_CHUNK = 128
Q_ROWS = 64
NA_GROUP = 16


def _cparams(semantics):
    return pltpu.CompilerParams(dimension_semantics=semantics,
                                vmem_limit_bytes=VMEM_LIMIT_BYTES)


def _tile_flags(seqs, n_tokens, tm):
    starts = {s for s, _ in seqs}
    ends = {e for _, e in seqs}
    nt = n_tokens // tm
    first = np.array([1 if i * tm in starts else 0 for i in range(nt)], np.int32)
    last = np.array([1 if (i + 1) * tm in ends else 0 for i in range(nt)], np.int32)
    return jnp.asarray(first), jnp.asarray(last)


def _rms_scale(x):
    return lax.rsqrt(jnp.mean(x * x, axis=-1, keepdims=True) + EPS)


def _inproj_kernel(x_ref, g_ref, w_ref, ov_ref, oqkv_ref, ol_ref, h_ref):
    j = pl.program_id(1)

    @pl.when(j == 0)
    def _():
        x = x_ref[...]
        h_ref[...] = (x * _rms_scale(x) * g_ref[...]).astype(BF16)

    r = jnp.dot(h_ref[...], w_ref[...], preferred_element_type=F32)

    @pl.when(j == 0)
    def _():
        ov_ref[...] = r[:, :D_CONV] * jax.nn.sigmoid(r[:, D_CONV:])

    @pl.when((j >= 1) & (j <= 3))
    def _():
        oqkv_ref[...] = r.astype(BF16)

    @pl.when(j == 4)
    def _():
        ol_ref[...] = r


def _inproj(x, g, w_bf16, layer):
    n = x.shape[0]
    tm, tn = TM_PROJ, TN_PROJ
    assert n % tm == 0 and D_IN == 5 * tn and 2 * D_CONV == tn and D_ATT == tn
    return pl.pallas_call(
        _inproj_kernel,
        grid=(n // tm, D_IN // tn),
        in_specs=[
            pl.BlockSpec((tm, D_MODEL), lambda i, j: (i, 0)),
            pl.BlockSpec((1, D_MODEL), lambda i, j: (0, 0)),
            pl.BlockSpec((None, D_MODEL, tn), lambda i, j: (layer, 0, j)),
        ],
        out_specs=[
            pl.BlockSpec((tm, D_CONV), lambda i, j: (i, 0)),
            pl.BlockSpec((tm, tn), lambda i, j: (i, jnp.clip(j - 1, 0, 2))),
            pl.BlockSpec((tm, tn), lambda i, j: (i, 0)),
        ],
        out_shape=[
            jax.ShapeDtypeStruct((n, D_CONV), F32),
            jax.ShapeDtypeStruct((n, 3 * D_ATT), BF16),
            jax.ShapeDtypeStruct((n, 2 * D_LRU), F32),
        ],
        scratch_shapes=[pltpu.VMEM((tm, D_MODEL), BF16)],
        compiler_params=_cparams(("parallel", "arbitrary")),
        name="inproj",
    )(x, g.reshape(1, D_MODEL), w_bf16)


def _conv_kernel(first_ref, last_ref, vm_ref, vp_ref, vn_ref,
                 w_ref, b_ref, lg_ref, lb_ref, o_ref, vext_ref, *, tm):
    i = pl.program_id(0)
    halo = CONV_HALO
    vext_ref[0:halo, :] = jnp.where(first_ref[i] == 1, 0.0, vp_ref[...])
    vext_ref[halo:halo + tm, :] = vm_ref[...]
    vext_ref[halo + tm:2 * halo + tm, :] = jnp.where(last_ref[i] == 1, 0.0, vn_ref[...])

    ch = CONV_CHUNK
    sub = SUBLANES_F32
    lead = halo - CONV_WIDTH // 2

    def body(c, carry):
        r0 = pl.multiple_of(c * ch, ch)
        for col in range(N_CONV_GROUPS):
            cs = slice(col * CONV_GROUP, (col + 1) * CONV_GROUP)
            acc = jnp.broadcast_to(b_ref[:, cs], (ch, CONV_GROUP))
            for s in range(sub):
                rows = ch + sub if s else ch
                part = None
                for k in range(CONV_WIDTH):
                    if (lead + k) % sub != s:
                        continue
                    term = w_ref[k:k + 1, cs] * vext_ref[pl.ds(r0 + (lead + k - s), rows), cs]
                    part = term if part is None else part + term
                acc = acc + part[s:s + ch, :]
            mu = jnp.mean(acc, axis=-1, keepdims=True)
            d = acc - mu
            var = jnp.mean(d * d, axis=-1, keepdims=True)
            y = d * lax.rsqrt(var + EPS) * lg_ref[:, cs] + lb_ref[:, cs]
            o_ref[pl.ds(r0, ch), cs] = (y * jax.nn.sigmoid(y)).astype(o_ref.dtype)
        return carry

    lax.fori_loop(0, tm // ch, body, 0, unroll=True)


def _conformer_conv(v, seqs, dw_w, dw_b, ln_g, ln_b):
    n = v.shape[0]
    tm, halo = TM_SEQ, CONV_HALO
    nt = n // tm
    hb = tm // halo
    nhb = n // halo
    first, last = _tile_flags(seqs, n, tm)

    def const(shape):
        return pl.BlockSpec(shape, lambda i, f, l: (0, 0))

    grid_spec = pltpu.PrefetchScalarGridSpec(
        num_scalar_prefetch=2,
        grid=(nt,),
        in_specs=[pl.BlockSpec((tm, D_CONV), lambda i, f, l: (i, 0)),
                  pl.BlockSpec((halo, D_CONV), lambda i, f, l: (jnp.maximum(i * hb - 1, 0), 0)),
                  pl.BlockSpec((halo, D_CONV), lambda i, f, l: (jnp.minimum((i + 1) * hb, nhb - 1), 0)),
                  const((CONV_WIDTH, D_CONV)), const((1, D_CONV)), const((1, D_CONV)), const((1, D_CONV))],
        out_specs=pl.BlockSpec((tm, D_CONV), lambda i, f, l: (i, 0)),
        scratch_shapes=[pltpu.VMEM((tm + 2 * halo, D_CONV), F32)],
    )
    return pl.pallas_call(
        functools.partial(_conv_kernel, tm=tm),
        grid_spec=grid_spec,
        out_shape=jax.ShapeDtypeStruct((n, D_CONV), BF16),
        compiler_params=_cparams(("parallel",)),
        name="conformer_conv",
    )(first, last, v, v, v,
      dw_w, dw_b.reshape(1, D_CONV), ln_g.reshape(1, D_CONV), ln_b.reshape(1, D_CONV))


N_BIAS_PAIRS = 2 * WIN_ROWS - 2


def _bias_pairs_kernel(rb_ref, o_ref):
    h = pl.program_id(0)
    n_dc = 2 * WIN_COLS - 1
    n_dr = 2 * WIN_ROWS - 1
    jj = lax.broadcasted_iota(jnp.int32, (GRID_W, 2 * GRID_W), 0)
    ll = lax.broadcasted_iota(jnp.int32, (GRID_W, 2 * GRID_W), 1)
    cc = ll & (GRID_W - 1)
    hi = ll >= GRID_W
    dc = cc - jj + (WIN_COLS - 1)
    cs = jnp.clip(jj - WIN_COLS // 2, 0, GRID_W - WIN_COLS)
    in_win = (cc >= cs) & (cc < cs + WIN_COLS)
    for a in range(N_BIAS_PAIRS):
        acc = jnp.zeros((GRID_W, 2 * GRID_W), F32)
        for d in range(n_dc):
            lo_v = rb_ref[(h * n_dr + a) * n_dc + d]
            hi_v = rb_ref[(h * n_dr + a + 1) * n_dc + d]
            acc = jnp.where(dc == d, jnp.where(hi, hi_v, lo_v), acc)
        o_ref[0, a] = jnp.where(in_win, acc, NEG_BIG)


def _bias_pairs(rel_bias):
    return pl.pallas_call(
        _bias_pairs_kernel,
        grid=(N_HEADS,),
        in_specs=[pl.BlockSpec(memory_space=pltpu.SMEM)],
        out_specs=pl.BlockSpec((1, N_BIAS_PAIRS, GRID_W, 2 * GRID_W), lambda h: (h, 0, 0, 0)),
        out_shape=jax.ShapeDtypeStruct((N_HEADS, N_BIAS_PAIRS, GRID_W, 2 * GRID_W), F32),
        compiler_params=_cparams(("parallel",)),
        name="na_bias_pairs",
    )(rel_bias.reshape(-1))


def _na_kernel(rs_ref, d0_ref, q_ref, k_ref, v_ref, p_ref, o_ref):
    blk = pl.program_id(1)
    scale = HEAD_DIM ** -0.5
    n_keys = WIN_ROWS * GRID_W
    for g0 in range(0, Q_ROWS, NA_GROUP):
        rows = range(g0, g0 + NA_GROUP)
        starts = {r: pl.multiple_of(rs_ref[blk * Q_ROWS + r] * GRID_W, GRID_W) for r in rows}
        d0s = {r: d0_ref[blk * Q_ROWS + r] for r in rows}
        scores = {}
        for r in rows:
            q = q_ref[r * GRID_W:(r + 1) * GRID_W, :]
            k = k_ref[pl.ds(starts[r], n_keys), :]
            scores[r] = lax.dot_general(q, k, (((1,), (1,)), ((), ())), preferred_element_type=F32)
        probs, denoms = {}, {}
        for r in rows:
            bias = jnp.concatenate([p_ref[0, d0s[r] + 2 * t] for t in range(WIN_ROWS // 2)], axis=1)
            s = scores[r] * scale + bias
            m = jnp.max(s, axis=-1, keepdims=True)
            p = jnp.exp(s - m)
            denoms[r] = jnp.sum(p, axis=-1, keepdims=True)
            probs[r] = p.astype(BF16)
        for r in rows:
            v = v_ref[pl.ds(starts[r], n_keys), :]
            o = jnp.dot(probs[r], v, preferred_element_type=F32) / denoms[r]
            o_ref[r * GRID_W:(r + 1) * GRID_W, :] = o.astype(o_ref.dtype)


def _na_rows(seqs):
    rs, d0 = [], []
    for s, e in seqs:
        r0, r1 = s // GRID_W, e // GRID_W
        assert r1 - r0 >= WIN_ROWS
        for g in range(r0, r1):
            st = min(max(g - WIN_ROWS // 2, r0), r1 - WIN_ROWS)
            rs.append(st)
            d0.append(st - g + WIN_ROWS - 1)
    return jnp.asarray(np.array(rs, np.int32)), jnp.asarray(np.array(d0, np.int32))


def _neighbourhood_attention(qkv, seqs, bias_pairs):
    n = qkv.shape[0]
    tq = Q_ROWS * GRID_W
    assert n % tq == 0 and all((e - s) % tq == 0 for s, e in seqs)
    rs, d0 = _na_rows(seqs)
    grid_spec = pltpu.PrefetchScalarGridSpec(
        num_scalar_prefetch=2,
        grid=(N_HEADS, n // tq),
        in_specs=[
            pl.BlockSpec((tq, HEAD_DIM), lambda h, b, rs, d0: (b, h)),
            pl.BlockSpec((n, HEAD_DIM), lambda h, b, rs, d0: (0, N_HEADS + h)),
            pl.BlockSpec((n, HEAD_DIM), lambda h, b, rs, d0: (0, 2 * N_HEADS + h)),
            pl.BlockSpec((1, N_BIAS_PAIRS, GRID_W, 2 * GRID_W), lambda h, b, rs, d0: (h, 0, 0, 0)),
        ],
        out_specs=pl.BlockSpec((tq, HEAD_DIM), lambda h, b, rs, d0: (b, h)),
    )
    return pl.pallas_call(
        _na_kernel,
        grid_spec=grid_spec,
        out_shape=jax.ShapeDtypeStruct((n, D_ATT), BF16),
        compiler_params=_cparams(("parallel", "arbitrary")),
        name="neighbourhood_attention",
    )(rs, d0, qkv, qkv, qkv, bias_pairs)


def _log_sigmoid(x):
    return jnp.minimum(x, 0.0) - jnp.log1p(jnp.exp(-jnp.abs(x)))


def _chunk_scan(a, b, reverse):
    rows = a.shape[0]
    row = lax.broadcasted_iota(jnp.int32, a.shape, 0)
    d = 1
    while d < rows:
        if reverse:
            ok = row < rows - d
            a_sh = jnp.where(ok, pltpu.roll(a, rows - d, 0), 1.0)
            b_sh = jnp.where(ok, pltpu.roll(b, rows - d, 0), 0.0)
        else:
            ok = row >= d
            a_sh = jnp.where(ok, pltpu.roll(a, d, 0), 1.0)
            b_sh = jnp.where(ok, pltpu.roll(b, d, 0), 0.0)
        b = a * b_sh + b
        a = a * a_sh
        d *= 2
    return a, b


def _lru_kernel(first_ref, last_ref,
                xmf_ref, xpf_ref, xnf_ref, xmb_ref, xpb_ref, xnb_ref,
                cw_ref, cb_ref, w_ref, ba_ref, bx_ref, lam_ref,
                hf_ref, hb_ref,
                xext_ref, a_ref, b_ref, carry_ref, *, tm, nt):
    i = pl.program_id(0)
    halo = LRU_HALO
    lead = halo - LRU_CONV_WIDTH // 2

    @pl.when(i == 0)
    def _():
        carry_ref[...] = jnp.zeros_like(carry_ref)

    def gates(d, tile, xm_ref, xp_ref, xn_ref):
        xext_ref[0:halo, :] = jnp.where(first_ref[tile] == 1, 0.0, xp_ref[...])
        xext_ref[halo:halo + tm, :] = xm_ref[...]
        xext_ref[halo + tm:2 * halo + tm, :] = jnp.where(last_ref[tile] == 1, 0.0, xn_ref[...])
        xe = xext_ref[...]
        rows_e = tm + 2 * halo
        xc = jnp.broadcast_to(cb_ref[...], (tm, D_LRU))
        for k in range(LRU_CONV_WIDTH):
            shift = (halo - lead - k) % rows_e
            tap = pltpu.roll(xe, shift, 0) if shift else xe
            xc = xc + cw_ref[k:k + 1, :] * tap[halo:halo + tm, :]
        log_s = _log_sigmoid(lam_ref[d])
        for nb in range(N_LRU_BLOCKS):
            sl = slice(nb * LRU_BLOCK, (nb + 1) * LRU_BLOCK)
            xb = xc[:, sl]
            z = jnp.dot(xb.astype(BF16), w_ref[d, nb], preferred_element_type=F32)
            r = jax.nn.sigmoid(z[:, :LRU_BLOCK] + ba_ref[d][:, sl])
            ig = jax.nn.sigmoid(z[:, LRU_BLOCK:] + bx_ref[d][:, sl])
            log_a = LRU_C * r * log_s[:, sl]
            a = jnp.exp(log_a)
            a_ref[d, :, sl] = a
            b2 = -jnp.tanh(log_a) * (1.0 + a * a)
            b_ref[d, :, sl] = jnp.where(b2 > 0.0, b2 * lax.rsqrt(b2), 0.0) * (ig * xb)

    gates(0, i, xmf_ref, xpf_ref, xnf_ref)
    gates(1, nt - 1 - i, xmb_ref, xpb_ref, xnb_ref)

    rows = SUBLANES_F32
    nch = tm // rows
    cf0 = jnp.where(first_ref[i] == 1, 0.0, carry_ref[0:1, :])
    cb0 = jnp.where(last_ref[nt - 1 - i] == 1, 0.0, carry_ref[1:2, :])

    def body(c, carry):
        cf, cb = carry
        rf = pl.multiple_of(c * rows, rows)
        rb = pl.multiple_of((nch - 1 - c) * rows, rows)
        af, hf = _chunk_scan(a_ref[0, pl.ds(rf, rows), :], b_ref[0, pl.ds(rf, rows), :], False)
        ab, hb = _chunk_scan(a_ref[1, pl.ds(rb, rows), :], b_ref[1, pl.ds(rb, rows), :], True)
        hf = hf + af * cf
        hb = hb + ab * cb
        hf_ref[pl.ds(rf, rows), :] = hf
        hb_ref[pl.ds(rb, rows), :] = hb
        return hf[rows - 1:rows, :], hb[0:1, :]

    cf, cb = lax.fori_loop(0, nch, body, (cf0, cb0), unroll=8)
    carry_ref[0:1, :] = cf
    carry_ref[1:2, :] = cb


def _rglru(u_lru, seqs, conv_w, conv_b, w_gates, ba, bx, lam):
    n = u_lru.shape[0]
    tm, halo = TM_SEQ, LRU_HALO
    nt = n // tm
    hb = tm // halo
    nhb = n // halo
    first, last = _tile_flags(seqs, n, tm)

    def tile_of(i, rev):
        return nt - 1 - i if rev else i

    def main(rev):
        return pl.BlockSpec((tm, D_LRU), lambda i, f, l: (tile_of(i, rev), 1))

    def prev(rev):
        return pl.BlockSpec((halo, D_LRU), lambda i, f, l: (jnp.maximum(tile_of(i, rev) * hb - 1, 0), 1))

    def nxt(rev):
        return pl.BlockSpec((halo, D_LRU), lambda i, f, l: (jnp.minimum((tile_of(i, rev) + 1) * hb, nhb - 1), 1))

    def const(shape):
        return pl.BlockSpec(shape, lambda i, f, l: (0,) * len(shape))

    grid_spec = pltpu.PrefetchScalarGridSpec(
        num_scalar_prefetch=2,
        grid=(nt,),
        in_specs=[main(False), prev(False), nxt(False), main(True), prev(True), nxt(True),
                  const((LRU_CONV_WIDTH, D_LRU)), const((1, D_LRU)),
                  const((2, N_LRU_BLOCKS, LRU_BLOCK, 2 * LRU_BLOCK)),
                  const((2, 1, D_LRU)), const((2, 1, D_LRU)), const((2, 1, D_LRU))],
        out_specs=[pl.BlockSpec((tm, D_LRU), lambda i, f, l: (i, 0)),
                   pl.BlockSpec((tm, D_LRU), lambda i, f, l: (nt - 1 - i, 0))],
        scratch_shapes=[pltpu.VMEM((tm + 2 * halo, D_LRU), F32),
                        pltpu.VMEM((2, tm, D_LRU), F32),
                        pltpu.VMEM((2, tm, D_LRU), F32),
                        pltpu.VMEM((SUBLANES_F32, D_LRU), F32)],
    )
    return pl.pallas_call(
        functools.partial(_lru_kernel, tm=tm, nt=nt),
        grid_spec=grid_spec,
        out_shape=[jax.ShapeDtypeStruct((n, D_LRU), F32), jax.ShapeDtypeStruct((n, D_LRU), F32)],
        compiler_params=_cparams(("arbitrary",)),
        name="rglru",
    )(first, last, u_lru, u_lru, u_lru, u_lru, u_lru, u_lru,
      conv_w, conv_b.reshape(1, D_LRU), w_gates,
      ba.reshape(2, 1, D_LRU), bx.reshape(2, 1, D_LRU), lam.reshape(2, 1, D_LRU))


def _outproj_kernel(x_ref, ya_ref, yb_ref, g_ref, hf_ref, hb_ref, w_ref, gn_ref, x1_ref, h2_ref):
    yc = (jax.nn.gelu(g_ref[...], approximate=True) * (hf_ref[...] + hb_ref[...])).astype(BF16)
    acc = jnp.dot(ya_ref[...], w_ref[0:D_CONV, :], preferred_element_type=F32)
    acc = acc + jnp.dot(yb_ref[...], w_ref[D_CONV:D_CONV + D_ATT, :], preferred_element_type=F32)
    acc = acc + jnp.dot(yc, w_ref[D_CONV + D_ATT:, :], preferred_element_type=F32)
    x1 = x_ref[...] + acc
    x1_ref[...] = x1
    h2_ref[...] = (x1 * _rms_scale(x1) * gn_ref[...]).astype(BF16)


def _outproj(x, ya, yb, u_lru, hf, hb, w_bf16, g_ffn, layer):
    n = x.shape[0]
    tm = TM_SEQ

    def rows(width, col=0):
        return pl.BlockSpec((tm, width), lambda i: (i, col))

    return pl.pallas_call(
        _outproj_kernel,
        grid=(n // tm,),
        in_specs=[rows(D_MODEL), rows(D_CONV), rows(D_ATT), rows(D_LRU, 0), rows(D_LRU), rows(D_LRU),
                  pl.BlockSpec((None, D_MODEL, D_MODEL), lambda i: (layer, 0, 0)),
                  pl.BlockSpec((1, D_MODEL), lambda i: (0, 0))],
        out_specs=[rows(D_MODEL), rows(D_MODEL)],
        out_shape=[jax.ShapeDtypeStruct((n, D_MODEL), F32), jax.ShapeDtypeStruct((n, D_MODEL), BF16)],
        compiler_params=_cparams(("parallel",)),
        name="outproj",
    )(x, ya, yb, u_lru, hf, hb, w_bf16, g_ffn.reshape(1, D_MODEL))


def _ffn_kernel(first_ref, last_ref, hm_ref, hp_ref, hn_ref, x1_ref,
                wga_ref, wva_ref, cwga_ref, cwva_ref, cbga_ref, cbva_ref, wda_ref,
                wgb_ref, wvb_ref, cwgb_ref, cwvb_ref, cbgb_ref, cbvb_ref, wdb_ref,
                fg_ref, o_ref, hext_ref, *, tm, nf, final_norm):
    i = pl.program_id(0)
    j = pl.program_id(1)
    halo = FFN_HALO
    rows_ext = tm + halo

    @pl.when(j == 0)
    def _():
        row = lax.broadcasted_iota(jnp.int32, hp_ref.shape, 0)
        before = jnp.where(first_ref[i] == 1, jnp.zeros_like(hp_ref), hp_ref[...])
        after = jnp.where(last_ref[i] == 1, jnp.zeros_like(hn_ref), hn_ref[...])
        hext_ref[0:halo, :] = jnp.where(row == 0, after.astype(F32), before.astype(F32)).astype(BF16)
        hext_ref[halo:rows_ext, :] = hm_ref[...]
        o_ref[...] = x1_ref[...]

    hext = hext_ref[...]

    def conv3(w_ref, cw_ref, cb_ref):
        u = jnp.dot(hext, w_ref[...], preferred_element_type=F32)
        prev = pltpu.roll(u, 1, 0)[halo:, :]
        nxt = pltpu.roll(u, rows_ext - 1, 0)[halo:, :]
        return (cw_ref[0:1, :] * prev + cw_ref[1:2, :] * u[halo:, :] + cw_ref[2:3, :] * nxt
                + cb_ref[...])

    def down(wg_ref, wv_ref, cwg_ref, cwv_ref, cbg_ref, cbv_ref, wd_ref):
        gate = conv3(wg_ref, cwg_ref, cbg_ref)
        val = conv3(wv_ref, cwv_ref, cbv_ref)
        act = (gate * jax.nn.sigmoid(gate) * val).astype(BF16)
        return jnp.dot(act, wd_ref[...], preferred_element_type=F32)

    refs_a = (wga_ref, wva_ref, cwga_ref, cwva_ref, cbga_ref, cbva_ref, wda_ref)
    refs_b = (wgb_ref, wvb_ref, cwgb_ref, cwvb_ref, cbgb_ref, cbvb_ref, wdb_ref)
    n_pairs = nf // 2

    @pl.when(j < n_pairs)
    def _():
        o_ref[...] += down(*refs_a) + down(*refs_b)

    if nf % 2:
        @pl.when(j == n_pairs)
        def _():
            o_ref[...] += down(*refs_a)

    if final_norm:
        @pl.when(j == pl.num_programs(1) - 1)
        def _():
            x2 = o_ref[...]
            o_ref[...] = x2 * _rms_scale(x2) * fg_ref[...]


def _ffn(h2, x1, seqs, w_up_bf16, conv_w, conv_b, w_down_bf16, final_g, layer, final_norm):
    n = h2.shape[0]
    tm, tf, halo = TM_SEQ, TF_FFN, FFN_HALO
    nt = n // tm
    nf = D_FF // tf
    hb = tm // halo
    nhb = n // halo
    first, last = _tile_flags(seqs, n, tm)
    cb = conv_b.reshape(1, 2 * D_FF)

    def chunk_specs(which):
        def c(j):
            return jnp.minimum(2 * j + which, nf - 1)
        return [
            pl.BlockSpec((None, D_MODEL, tf), lambda i, j, f, l: (layer, 0, c(j))),
            pl.BlockSpec((None, D_MODEL, tf), lambda i, j, f, l: (layer, 0, nf + c(j))),
            pl.BlockSpec((3, tf), lambda i, j, f, l: (0, c(j))),
            pl.BlockSpec((3, tf), lambda i, j, f, l: (0, nf + c(j))),
            pl.BlockSpec((1, tf), lambda i, j, f, l: (0, c(j))),
            pl.BlockSpec((1, tf), lambda i, j, f, l: (0, nf + c(j))),
            pl.BlockSpec((None, tf, D_MODEL), lambda i, j, f, l: (layer, c(j), 0)),
        ]

    grid_spec = pltpu.PrefetchScalarGridSpec(
        num_scalar_prefetch=2,
        grid=(nt, (nf + 1) // 2),
        in_specs=[
            pl.BlockSpec((tm, D_MODEL), lambda i, j, f, l: (i, 0)),
            pl.BlockSpec((halo, D_MODEL), lambda i, j, f, l: (jnp.maximum(i * hb - 1, 0), 0)),
            pl.BlockSpec((halo, D_MODEL), lambda i, j, f, l: (jnp.minimum((i + 1) * hb, nhb - 1), 0)),
            pl.BlockSpec((tm, D_MODEL), lambda i, j, f, l: (i, 0)),
        ] + chunk_specs(0) + chunk_specs(1) + [
            pl.BlockSpec((1, D_MODEL), lambda i, j, f, l: (0, 0)),
        ],
        out_specs=pl.BlockSpec((tm, D_MODEL), lambda i, j, f, l: (i, 0)),
        scratch_shapes=[pltpu.VMEM((tm + halo, D_MODEL), BF16)],
    )
    chunk_args = (w_up_bf16, w_up_bf16, conv_w, conv_w, cb, cb, w_down_bf16)
    return pl.pallas_call(
        functools.partial(_ffn_kernel, tm=tm, nf=nf, final_norm=final_norm),
        grid_spec=grid_spec,
        out_shape=jax.ShapeDtypeStruct((n, D_MODEL), F32),
        compiler_params=_cparams(("parallel", "arbitrary")),
        name="ffn_final" if final_norm else "ffn",
    )(first, last, h2, h2, h2, x1, *chunk_args, *chunk_args, final_g.reshape(1, D_MODEL))


def _trunk(x, seqs, p):
    depth = p["w_in"].shape[0]
    for l in range(depth):
        v_conv, qkv, u_lru = _inproj(x, p["norm_mix_g"][l], p["w_in"], l)
        ya = _conformer_conv(v_conv, seqs, p["conv_dw_w"][l], p["conv_dw_b"][l], p["conv_ln_g"][l], p["conv_ln_b"][l])
        yb = _neighbourhood_attention(qkv, seqs, p["bias_pairs"][l])
        hf, hb = _rglru(u_lru, seqs, p["lru_conv_w"][l], p["lru_conv_b"][l], p["lru_w_gates"][l],
                        p["lru_ba"][l], p["lru_bx"][l], p["lru_lambda"][l])
        x1, h2 = _outproj(x, ya, yb, u_lru, hf, hb, p["w_out"], p["norm_ffn_g"][l], l)
        x = _ffn(h2, x1, seqs, p["ffn_w_up"], p["ffn_conv_w"][l], p["ffn_conv_b"][l], p["ffn_w_down"],
                 p["final_norm_g"], l, final_norm=(l == depth - 1))
    return x


def kernel(x_prompt, x_sample, norm_mix_g, w_in, conv_dw_w, conv_dw_b, conv_ln_g, conv_ln_b, na_rel_bias, lru_conv_w, lru_conv_b, lru_wa, lru_ba, lru_wx, lru_bx, lru_lambda, w_out, norm_ffn_g, ffn_w_up, ffn_conv_w, ffn_conv_b, ffn_w_down, final_norm_g):
    depth = w_in.shape[0]
    p = dict(
        norm_mix_g=norm_mix_g, conv_dw_w=conv_dw_w, conv_dw_b=conv_dw_b, conv_ln_g=conv_ln_g, conv_ln_b=conv_ln_b,
        lru_conv_w=lru_conv_w, lru_conv_b=lru_conv_b, lru_ba=lru_ba, lru_bx=lru_bx, lru_lambda=lru_lambda,
        norm_ffn_g=norm_ffn_g, ffn_conv_w=ffn_conv_w, ffn_conv_b=ffn_conv_b, final_norm_g=final_norm_g,
        w_in=w_in.astype(BF16), w_out=w_out.astype(BF16),
        ffn_w_up=ffn_w_up.astype(BF16), ffn_w_down=ffn_w_down.astype(BF16),
        lru_w_gates=jnp.concatenate([lru_wa, lru_wx], axis=-1).astype(BF16),
        bias_pairs=[_bias_pairs(na_rel_bias[l]) for l in range(depth)],
    )
    outs = []
    for xb in (x_prompt, x_sample):
        b, t, _ = xb.shape
        seqs = tuple((s * t, (s + 1) * t) for s in range(b))
        outs.append(_trunk(xb.reshape(b * t, D_MODEL), seqs, p).reshape(xb.shape))
    return tuple(outs)
```

```python
import functools

import numpy as np
import jax
import jax.numpy as jnp
from jax import lax
from jax.experimental import pallas as pl
from jax.experimental.pallas import tpu as pltpu

F32 = jnp.float32
BF16 = jnp.bfloat16

D_MODEL = 2048
GRID_W = 64
D_CONV = 512
D_ATT = 1024
D_LRU = 512
N_CONV_GROUPS = 4
CONV_GROUP = 128
CONV_WIDTH = 31
HEAD_DIM = 128
N_HEADS = 8
WIN_ROWS = 8
WIN_COLS = 16
N_LRU_BLOCKS = 4
LRU_BLOCK = 128
LRU_CONV_WIDTH = 4
LRU_C = 8.0
D_FF = 5632
D_IN = 5120
EPS = 1e-6
NEG_BIG = -1e30

SUBLANES_F32 = 8
SUBLANES_BF16 = 16
VMEM_LIMIT_BYTES = 56 * 1024 * 1024

TM_PROJ = 1024
TN_PROJ = 1024
TM_SEQ = 512
TF_FFN = 512
CONV_HALO = 16
LRU_HALO = 8
FFN_HALO = SUBLANES_BF16
CONV_CHUNK = 128
Q_ROWS = 64
NA_GROUP = 16


def _cparams(semantics):
    return pltpu.CompilerParams(dimension_semantics=semantics,
                                vmem_limit_bytes=VMEM_LIMIT_BYTES)


def _tile_flags(seqs, n_tokens, tm):
    starts = {s for s, _ in seqs}
    ends = {e for _, e in seqs}
    nt = n_tokens // tm
    first = np.array([1 if i * tm in starts else 0 for i in range(nt)], np.int32)
    last = np.array([1 if (i + 1) * tm in ends else 0 for i in range(nt)], np.int32)
    return jnp.asarray(first), jnp.asarray(last)


def _rms_scale(x):
    return lax.rsqrt(jnp.mean(x * x, axis=-1, keepdims=True) + EPS)


def _inproj_kernel(x_ref, g_ref, w_ref, ov_ref, oqkv_ref, ol_ref):
    x = x_ref[...]
    h = (x * _rms_scale(x) * g_ref[...]).astype(BF16)
    tn = TN_PROJ

    def proj(c):
        return jnp.dot(h, w_ref[:, c * tn:(c + 1) * tn], preferred_element_type=F32)

    r = proj(0)
    ov_ref[...] = r[:, :D_CONV] * jax.nn.sigmoid(r[:, D_CONV:])
    for c in range(1, 4):
        oqkv_ref[:, (c - 1) * tn:c * tn] = proj(c).astype(BF16)
    ol_ref[...] = proj(4)


def _inproj(x, g, w_bf16, layer):
    n = x.shape[0]
    tm, tn = TM_SEQ, TN_PROJ
    assert n % tm == 0 and D_IN == 5 * tn and 2 * D_CONV == tn and D_ATT == tn
    return pl.pallas_call(
        _inproj_kernel,
        grid=(n // tm,),
        in_specs=[
            pl.BlockSpec((tm, D_MODEL), lambda i: (i, 0)),
            pl.BlockSpec((1, D_MODEL), lambda i: (0, 0)),
            pl.BlockSpec((None, D_MODEL, D_IN), lambda i: (layer, 0, 0), pipeline_mode=pl.Buffered(1)),
        ],
        out_specs=[
            pl.BlockSpec((tm, D_CONV), lambda i: (i, 0)),
            pl.BlockSpec((tm, 3 * D_ATT), lambda i: (i, 0)),
            pl.BlockSpec((tm, 2 * D_LRU), lambda i: (i, 0)),
        ],
        out_shape=[
            jax.ShapeDtypeStruct((n, D_CONV), F32),
            jax.ShapeDtypeStruct((n, 3 * D_ATT), BF16),
            jax.ShapeDtypeStruct((n, 2 * D_LRU), F32),
        ],
        compiler_params=_cparams(("parallel",)),
        name="inproj",
    )(x, g.reshape(1, D_MODEL), w_bf16)


def _conv_kernel(first_ref, last_ref, vm_ref, vp_ref, vn_ref,
                 w_ref, b_ref, lg_ref, lb_ref, o_ref, vext_ref, *, tm):
    i = pl.program_id(0)
    halo = CONV_HALO
    vext_ref[0:halo, :] = jnp.where(first_ref[i] == 1, 0.0, vp_ref[...])
    vext_ref[halo:halo + tm, :] = vm_ref[...]
    vext_ref[halo + tm:2 * halo + tm, :] = jnp.where(last_ref[i] == 1, 0.0, vn_ref[...])

    ch = CONV_CHUNK
    sub = SUBLANES_F32
    lead = halo - CONV_WIDTH // 2

    def body(c, carry):
        r0 = pl.multiple_of(c * ch, ch)
        for col in range(N_CONV_GROUPS):
            cs = slice(col * CONV_GROUP, (col + 1) * CONV_GROUP)
            acc = jnp.broadcast_to(b_ref[:, cs], (ch, CONV_GROUP))
            for s in range(sub):
                rows = ch + sub if s else ch
                part = None
                for k in range(CONV_WIDTH):
                    if (lead + k) % sub != s:
                        continue
                    term = w_ref[k:k + 1, cs] * vext_ref[pl.ds(r0 + (lead + k - s), rows), cs]
                    part = term if part is None else part + term
                acc = acc + part[s:s + ch, :]
            mu = jnp.mean(acc, axis=-1, keepdims=True)
            d = acc - mu
            var = jnp.mean(d * d, axis=-1, keepdims=True)
            y = d * lax.rsqrt(var + EPS) * lg_ref[:, cs] + lb_ref[:, cs]
            o_ref[pl.ds(r0, ch), cs] = (y * jax.nn.sigmoid(y)).astype(o_ref.dtype)
        return carry

    lax.fori_loop(0, tm // ch, body, 0, unroll=True)


def _conformer_conv(v, seqs, dw_w, dw_b, ln_g, ln_b):
    n = v.shape[0]
    tm, halo = TM_SEQ, CONV_HALO
    nt = n // tm
    hb = tm // halo
    nhb = n // halo
    first, last = _tile_flags(seqs, n, tm)

    def const(shape):
        return pl.BlockSpec(shape, lambda i, f, l: (0, 0))

    grid_spec = pltpu.PrefetchScalarGridSpec(
        num_scalar_prefetch=2,
        grid=(nt,),
        in_specs=[pl.BlockSpec((tm, D_CONV), lambda i, f, l: (i, 0)),
                  pl.BlockSpec((halo, D_CONV), lambda i, f, l: (jnp.maximum(i * hb - 1, 0), 0)),
                  pl.BlockSpec((halo, D_CONV), lambda i, f, l: (jnp.minimum((i + 1) * hb, nhb - 1), 0)),
                  const((CONV_WIDTH, D_CONV)), const((1, D_CONV)), const((1, D_CONV)), const((1, D_CONV))],
        out_specs=pl.BlockSpec((tm, D_CONV), lambda i, f, l: (i, 0)),
        scratch_shapes=[pltpu.VMEM((tm + 2 * halo, D_CONV), F32)],
    )
    return pl.pallas_call(
        functools.partial(_conv_kernel, tm=tm),
        grid_spec=grid_spec,
        out_shape=jax.ShapeDtypeStruct((n, D_CONV), BF16),
        compiler_params=_cparams(("parallel",)),
        name="conformer_conv",
    )(first, last, v, v, v,
      dw_w, dw_b.reshape(1, D_CONV), ln_g.reshape(1, D_CONV), ln_b.reshape(1, D_CONV))


N_BIAS_PAIRS = 2 * WIN_ROWS - 2


def _bias_pairs_kernel(rb_ref, o_ref):
    h = pl.program_id(0)
    n_dc = 2 * WIN_COLS - 1
    n_dr = 2 * WIN_ROWS - 1
    jj = lax.broadcasted_iota(jnp.int32, (GRID_W, 2 * GRID_W), 0)
    ll = lax.broadcasted_iota(jnp.int32, (GRID_W, 2 * GRID_W), 1)
    cc = ll & (GRID_W - 1)
    hi = ll >= GRID_W
    dc = cc - jj + (WIN_COLS - 1)
    cs = jnp.clip(jj - WIN_COLS // 2, 0, GRID_W - WIN_COLS)
    in_win = (cc >= cs) & (cc < cs + WIN_COLS)
    for a in range(N_BIAS_PAIRS):
        acc = jnp.zeros((GRID_W, 2 * GRID_W), F32)
        for d in range(n_dc):
            lo_v = rb_ref[(h * n_dr + a) * n_dc + d]
            hi_v = rb_ref[(h * n_dr + a + 1) * n_dc + d]
            acc = jnp.where(dc == d, jnp.where(hi, hi_v, lo_v), acc)
        o_ref[0, a] = jnp.where(in_win, acc, NEG_BIG)


def _bias_pairs(rel_bias):
    return pl.pallas_call(
        _bias_pairs_kernel,
        grid=(N_HEADS,),
        in_specs=[pl.BlockSpec(memory_space=pltpu.SMEM)],
        out_specs=pl.BlockSpec((1, N_BIAS_PAIRS, GRID_W, 2 * GRID_W), lambda h: (h, 0, 0, 0)),
        out_shape=jax.ShapeDtypeStruct((N_HEADS, N_BIAS_PAIRS, GRID_W, 2 * GRID_W), F32),
        compiler_params=_cparams(("parallel",)),
        name="na_bias_pairs",
    )(rel_bias.reshape(-1))


def _na_kernel(rs_ref, d0_ref, q_ref, k_ref, v_ref, p_ref, o_ref):
    blk = pl.program_id(1)
    scale = HEAD_DIM ** -0.5
    n_keys = WIN_ROWS * GRID_W
    for g0 in range(0, Q_ROWS, NA_GROUP):
        rows = range(g0, g0 + NA_GROUP)
        starts = {r: pl.multiple_of(rs_ref[blk * Q_ROWS + r] * GRID_W, GRID_W) for r in rows}
        d0s = {r: d0_ref[blk * Q_ROWS + r] for r in rows}
        scores = {}
        for r in rows:
            q = q_ref[r * GRID_W:(r + 1) * GRID_W, :]
            k = k_ref[pl.ds(starts[r], n_keys), :]
            scores[r] = lax.dot_general(q, k, (((1,), (1,)), ((), ())), preferred_element_type=F32)
        probs, denoms = {}, {}
        for r in rows:
            bias = jnp.concatenate([p_ref[0, d0s[r] + 2 * t] for t in range(WIN_ROWS // 2)], axis=1)
            s = scores[r] * scale + bias
            m = jnp.max(s, axis=-1, keepdims=True)
            p = jnp.exp(s - m)
            denoms[r] = jnp.sum(p, axis=-1, keepdims=True)
            probs[r] = p.astype(BF16)
        for r in rows:
            v = v_ref[pl.ds(starts[r], n_keys), :]
            o = jnp.dot(probs[r], v, preferred_element_type=F32) / denoms[r]
            o_ref[r * GRID_W:(r + 1) * GRID_W, :] = o.astype(o_ref.dtype)


def _na_rows(seqs):
    rs, d0 = [], []
    for s, e in seqs:
        r0, r1 = s // GRID_W, e // GRID_W
        assert r1 - r0 >= WIN_ROWS
        for g in range(r0, r1):
            st = min(max(g - WIN_ROWS // 2, r0), r1 - WIN_ROWS)
            rs.append(st)
            d0.append(st - g + WIN_ROWS - 1)
    return jnp.asarray(np.array(rs, np.int32)), jnp.asarray(np.array(d0, np.int32))


def _neighbourhood_attention(qkv, seqs, bias_pairs):
    n = qkv.shape[0]
    tq = Q_ROWS * GRID_W
    assert n % tq == 0 and all((e - s) % tq == 0 for s, e in seqs)
    rs, d0 = _na_rows(seqs)
    grid_spec = pltpu.PrefetchScalarGridSpec(
        num_scalar_prefetch=2,
        grid=(N_HEADS, n // tq),
        in_specs=[
            pl.BlockSpec((tq, HEAD_DIM), lambda h, b, rs, d0: (b, h)),
            pl.BlockSpec((n, HEAD_DIM), lambda h, b, rs, d0: (0, N_HEADS + h)),
            pl.BlockSpec((n, HEAD_DIM), lambda h, b, rs, d0: (0, 2 * N_HEADS + h)),
            pl.BlockSpec((1, N_BIAS_PAIRS, GRID_W, 2 * GRID_W), lambda h, b, rs, d0: (h, 0, 0, 0)),
        ],
        out_specs=pl.BlockSpec((tq, HEAD_DIM), lambda h, b, rs, d0: (b, h)),
    )
    return pl.pallas_call(
        _na_kernel,
        grid_spec=grid_spec,
        out_shape=jax.ShapeDtypeStruct((n, D_ATT), BF16),
        compiler_params=_cparams(("parallel", "arbitrary")),
        name="neighbourhood_attention",
    )(rs, d0, qkv, qkv, qkv, bias_pairs)


def _log_sigmoid(x):
    return jnp.minimum(x, 0.0) - jnp.log1p(jnp.exp(-jnp.abs(x)))


def _chunk_scan(a, b, reverse):
    rows = a.shape[0]
    row = lax.broadcasted_iota(jnp.int32, a.shape, 0)
    d = 1
    while d < rows:
        if reverse:
            ok = row < rows - d
            a_sh = jnp.where(ok, pltpu.roll(a, rows - d, 0), 1.0)
            b_sh = jnp.where(ok, pltpu.roll(b, rows - d, 0), 0.0)
        else:
            ok = row >= d
            a_sh = jnp.where(ok, pltpu.roll(a, d, 0), 1.0)
            b_sh = jnp.where(ok, pltpu.roll(b, d, 0), 0.0)
        b = a * b_sh + b
        a = a * a_sh
        d *= 2
    return a, b


def _lru_kernel(first_ref, last_ref,
                xmf_ref, xpf_ref, xnf_ref, xmb_ref, xpb_ref, xnb_ref,
                cw_ref, cb_ref, w_ref, ba_ref, bx_ref, lam_ref,
                hf_ref, hb_ref,
                xext_ref, a_ref, b_ref, carry_ref, *, tm, nt):
    i = pl.program_id(0)
    halo = LRU_HALO
    lead = halo - LRU_CONV_WIDTH // 2

    @pl.when(i == 0)
    def _():
        carry_ref[...] = jnp.zeros_like(carry_ref)

    def gates(d, tile, xm_ref, xp_ref, xn_ref):
        xext_ref[0:halo, :] = jnp.where(first_ref[tile] == 1, 0.0, xp_ref[...])
        xext_ref[halo:halo + tm, :] = xm_ref[...]
        xext_ref[halo + tm:2 * halo + tm, :] = jnp.where(last_ref[tile] == 1, 0.0, xn_ref[...])
        xe = xext_ref[...]
        rows_e = tm + 2 * halo
        xc = jnp.broadcast_to(cb_ref[...], (tm, D_LRU))
        for k in range(LRU_CONV_WIDTH):
            shift = (halo - lead - k) % rows_e
            tap = pltpu.roll(xe, shift, 0) if shift else xe
            xc = xc + cw_ref[k:k + 1, :] * tap[halo:halo + tm, :]
        log_s = _log_sigmoid(lam_ref[d])
        for nb in range(N_LRU_BLOCKS):
            sl = slice(nb * LRU_BLOCK, (nb + 1) * LRU_BLOCK)
            xb = xc[:, sl]
            z = jnp.dot(xb.astype(BF16), w_ref[d, nb], preferred_element_type=F32)
            r = jax.nn.sigmoid(z[:, :LRU_BLOCK] + ba_ref[d][:, sl])
            ig = jax.nn.sigmoid(z[:, LRU_BLOCK:] + bx_ref[d][:, sl])
            log_a = LRU_C * r * log_s[:, sl]
            a = jnp.exp(log_a)
            a_ref[d, :, sl] = a
            b2 = -jnp.tanh(log_a) * (1.0 + a * a)
            b_ref[d, :, sl] = jnp.where(b2 > 0.0, b2 * lax.rsqrt(b2), 0.0) * (ig * xb)

    gates(0, i, xmf_ref, xpf_ref, xnf_ref)
    gates(1, nt - 1 - i, xmb_ref, xpb_ref, xnb_ref)

    rows = SUBLANES_F32
    nch = tm // rows
    cf0 = jnp.where(first_ref[i] == 1, 0.0, carry_ref[0:1, :])
    cb0 = jnp.where(last_ref[nt - 1 - i] == 1, 0.0, carry_ref[1:2, :])

    def body(c, carry):
        cf, cb = carry
        rf = pl.multiple_of(c * rows, rows)
        rb = pl.multiple_of((nch - 1 - c) * rows, rows)
        af, hf = _chunk_scan(a_ref[0, pl.ds(rf, rows), :], b_ref[0, pl.ds(rf, rows), :], False)
        ab, hb = _chunk_scan(a_ref[1, pl.ds(rb, rows), :], b_ref[1, pl.ds(rb, rows), :], True)
        hf = hf + af * cf
        hb = hb + ab * cb
        hf_ref[pl.ds(rf, rows), :] = hf
        hb_ref[pl.ds(rb, rows), :] = hb
        return hf[rows - 1:rows, :], hb[0:1, :]

    cf, cb = lax.fori_loop(0, nch, body, (cf0, cb0), unroll=8)
    carry_ref[0:1, :] = cf
    carry_ref[1:2, :] = cb


def _rglru(u_lru, seqs, conv_w, conv_b, w_gates, ba, bx, lam):
    n = u_lru.shape[0]
    tm, halo = TM_SEQ, LRU_HALO
    nt = n // tm
    hb = tm // halo
    nhb = n // halo
    first, last = _tile_flags(seqs, n, tm)

    def tile_of(i, rev):
        return nt - 1 - i if rev else i

    def main(rev):
        return pl.BlockSpec((tm, D_LRU), lambda i, f, l: (tile_of(i, rev), 1))

    def prev(rev):
        return pl.BlockSpec((halo, D_LRU), lambda i, f, l: (jnp.maximum(tile_of(i, rev) * hb - 1, 0), 1))

    def nxt(rev):
        return pl.BlockSpec((halo, D_LRU), lambda i, f, l: (jnp.minimum((tile_of(i, rev) + 1) * hb, nhb - 1), 1))

    def const(shape):
        return pl.BlockSpec(shape, lambda i, f, l: (0,) * len(shape))

    grid_spec = pltpu.PrefetchScalarGridSpec(
        num_scalar_prefetch=2,
        grid=(nt,),
        in_specs=[main(False), prev(False), nxt(False), main(True), prev(True), nxt(True),
                  const((LRU_CONV_WIDTH, D_LRU)), const((1, D_LRU)),
                  const((2, N_LRU_BLOCKS, LRU_BLOCK, 2 * LRU_BLOCK)),
                  const((2, 1, D_LRU)), const((2, 1, D_LRU)), const((2, 1, D_LRU))],
        out_specs=[pl.BlockSpec((tm, D_LRU), lambda i, f, l: (i, 0)),
                   pl.BlockSpec((tm, D_LRU), lambda i, f, l: (nt - 1 - i, 0))],
        scratch_shapes=[pltpu.VMEM((tm + 2 * halo, D_LRU), F32),
                        pltpu.VMEM((2, tm, D_LRU), F32),
                        pltpu.VMEM((2, tm, D_LRU), F32),
                        pltpu.VMEM((SUBLANES_F32, D_LRU), F32)],
    )
    return pl.pallas_call(
        functools.partial(_lru_kernel, tm=tm, nt=nt),
        grid_spec=grid_spec,
        out_shape=[jax.ShapeDtypeStruct((n, D_LRU), F32), jax.ShapeDtypeStruct((n, D_LRU), F32)],
        compiler_params=_cparams(("arbitrary",)),
        name="rglru",
    )(first, last, u_lru, u_lru, u_lru, u_lru, u_lru, u_lru,
      conv_w, conv_b.reshape(1, D_LRU), w_gates,
      ba.reshape(2, 1, D_LRU), bx.reshape(2, 1, D_LRU), lam.reshape(2, 1, D_LRU))


def _outproj_kernel(x_ref, ya_ref, yb_ref, g_ref, hf_ref, hb_ref, w_ref, gn_ref, x1_ref, h2_ref):
    yc = (jax.nn.gelu(g_ref[...], approximate=True) * (hf_ref[...] + hb_ref[...])).astype(BF16)
    acc = jnp.dot(ya_ref[...], w_ref[0:D_CONV, :], preferred_element_type=F32)
    acc = acc + jnp.dot(yb_ref[...], w_ref[D_CONV:D_CONV + D_ATT, :], preferred_element_type=F32)
    acc = acc + jnp.dot(yc, w_ref[D_CONV + D_ATT:, :], preferred_element_type=F32)
    x1 = x_ref[...] + acc
    x1_ref[...] = x1
    h2_ref[...] = (x1 * _rms_scale(x1) * gn_ref[...]).astype(BF16)


def _outproj(x, ya, yb, u_lru, hf, hb, w_bf16, g_ffn, layer):
    n = x.shape[0]
    tm = TM_SEQ

    def rows(width, col=0):
        return pl.BlockSpec((tm, width), lambda i: (i, col))

    return pl.pallas_call(
        _outproj_kernel,
        grid=(n // tm,),
        in_specs=[rows(D_MODEL), rows(D_CONV), rows(D_ATT), rows(D_LRU, 0), rows(D_LRU), rows(D_LRU),
                  pl.BlockSpec((None, D_MODEL, D_MODEL), lambda i: (layer, 0, 0)),
                  pl.BlockSpec((1, D_MODEL), lambda i: (0, 0))],
        out_specs=[rows(D_MODEL), rows(D_MODEL)],
        out_shape=[jax.ShapeDtypeStruct((n, D_MODEL), F32), jax.ShapeDtypeStruct((n, D_MODEL), BF16)],
        compiler_params=_cparams(("parallel",)),
        name="outproj",
    )(x, ya, yb, u_lru, hf, hb, w_bf16, g_ffn.reshape(1, D_MODEL))


def _ffn_kernel(first_ref, last_ref, hm_ref, hp_ref, hn_ref, x1_ref,
                wga_ref, wva_ref, cwga_ref, cwva_ref, cbga_ref, cbva_ref, wda_ref,
                wgb_ref, wvb_ref, cwgb_ref, cwvb_ref, cbgb_ref, cbvb_ref, wdb_ref,
                fg_ref, o_ref, hext_ref, *, tm, nf, final_norm):
    i = pl.program_id(0)
    j = pl.program_id(1)
    halo = FFN_HALO
    rows_ext = tm + halo

    @pl.when(j == 0)
    def _():
        row = lax.broadcasted_iota(jnp.int32, hp_ref.shape, 0)
        before = jnp.where(first_ref[i] == 1, jnp.zeros_like(hp_ref), hp_ref[...])
        after = jnp.where(last_ref[i] == 1, jnp.zeros_like(hn_ref), hn_ref[...])
        hext_ref[0:halo, :] = jnp.where(row == 0, after.astype(F32), before.astype(F32)).astype(BF16)
        hext_ref[halo:rows_ext, :] = hm_ref[...]
        o_ref[...] = x1_ref[...]

    hext = hext_ref[...]

    def conv3(w_ref, cw_ref, cb_ref):
        u = jnp.dot(hext, w_ref[...], preferred_element_type=F32)
        prev = pltpu.roll(u, 1, 0)[halo:, :]
        nxt = pltpu.roll(u, rows_ext - 1, 0)[halo:, :]
        return (cw_ref[0:1, :] * prev + cw_ref[1:2, :] * u[halo:, :] + cw_ref[2:3, :] * nxt
                + cb_ref[...])

    def down(wg_ref, wv_ref, cwg_ref, cwv_ref, cbg_ref, cbv_ref, wd_ref):
        gate = conv3(wg_ref, cwg_ref, cbg_ref)
        val = conv3(wv_ref, cwv_ref, cbv_ref)
        act = (gate * jax.nn.sigmoid(gate) * val).astype(BF16)
        return jnp.dot(act, wd_ref[...], preferred_element_type=F32)

    refs_a = (wga_ref, wva_ref, cwga_ref, cwva_ref, cbga_ref, cbva_ref, wda_ref)
    refs_b = (wgb_ref, wvb_ref, cwgb_ref, cwvb_ref, cbgb_ref, cbvb_ref, wdb_ref)
    n_pairs = nf // 2

    @pl.when(j < n_pairs)
    def _():
        o_ref[...] += down(*refs_a) + down(*refs_b)

    if nf % 2:
        @pl.when(j == n_pairs)
        def _():
            o_ref[...] += down(*refs_a)

    if final_norm:
        @pl.when(j == pl.num_programs(1) - 1)
        def _():
            x2 = o_ref[...]
            o_ref[...] = x2 * _rms_scale(x2) * fg_ref[...]


def _ffn(h2, x1, seqs, w_up_bf16, conv_w, conv_b, w_down_bf16, final_g, layer, final_norm):
    n = h2.shape[0]
    tm, tf, halo = TM_SEQ, TF_FFN, FFN_HALO
    nt = n // tm
    nf = D_FF // tf
    hb = tm // halo
    nhb = n // halo
    first, last = _tile_flags(seqs, n, tm)
    cb = conv_b.reshape(1, 2 * D_FF)

    def chunk_specs(which):
        def c(j):
            return jnp.minimum(2 * j + which, nf - 1)
        return [
            pl.BlockSpec((None, D_MODEL, tf), lambda i, j, f, l: (layer, 0, c(j))),
            pl.BlockSpec((None, D_MODEL, tf), lambda i, j, f, l: (layer, 0, nf + c(j))),
            pl.BlockSpec((3, tf), lambda i, j, f, l: (0, c(j))),
            pl.BlockSpec((3, tf), lambda i, j, f, l: (0, nf + c(j))),
            pl.BlockSpec((1, tf), lambda i, j, f, l: (0, c(j))),
            pl.BlockSpec((1, tf), lambda i, j, f, l: (0, nf + c(j))),
            pl.BlockSpec((None, tf, D_MODEL), lambda i, j, f, l: (layer, c(j), 0)),
        ]

    grid_spec = pltpu.PrefetchScalarGridSpec(
        num_scalar_prefetch=2,
        grid=(nt, (nf + 1) // 2),
        in_specs=[
            pl.BlockSpec((tm, D_MODEL), lambda i, j, f, l: (i, 0)),
            pl.BlockSpec((halo, D_MODEL), lambda i, j, f, l: (jnp.maximum(i * hb - 1, 0), 0)),
            pl.BlockSpec((halo, D_MODEL), lambda i, j, f, l: (jnp.minimum((i + 1) * hb, nhb - 1), 0)),
            pl.BlockSpec((tm, D_MODEL), lambda i, j, f, l: (i, 0)),
        ] + chunk_specs(0) + chunk_specs(1) + [
            pl.BlockSpec((1, D_MODEL), lambda i, j, f, l: (0, 0)),
        ],
        out_specs=pl.BlockSpec((tm, D_MODEL), lambda i, j, f, l: (i, 0)),
        scratch_shapes=[pltpu.VMEM((tm + halo, D_MODEL), BF16)],
    )
    chunk_args = (w_up_bf16, w_up_bf16, conv_w, conv_w, cb, cb, w_down_bf16)
    return pl.pallas_call(
        functools.partial(_ffn_kernel, tm=tm, nf=nf, final_norm=final_norm),
        grid_spec=grid_spec,
        out_shape=jax.ShapeDtypeStruct((n, D_MODEL), F32),
        compiler_params=_cparams(("parallel", "arbitrary")),
        name="ffn_final" if final_norm else "ffn",
    )(first, last, h2, h2, h2, x1, *chunk_args, *chunk_args, final_g.reshape(1, D_MODEL))


def _trunk(x, seqs, p):
    depth = p["w_in"].shape[0]
    for l in range(depth):
        v_conv, qkv, u_lru = _inproj(x, p["norm_mix_g"][l], p["w_in"], l)
        ya = _conformer_conv(v_conv, seqs, p["conv_dw_w"][l], p["conv_dw_b"][l], p["conv_ln_g"][l], p["conv_ln_b"][l])
        yb = _neighbourhood_attention(qkv, seqs, p["bias_pairs"][l])
        hf, hb = _rglru(u_lru, seqs, p["lru_conv_w"][l], p["lru_conv_b"][l], p["lru_w_gates"][l],
                        p["lru_ba"][l], p["lru_bx"][l], p["lru_lambda"][l])
        x1, h2 = _outproj(x, ya, yb, u_lru, hf, hb, p["w_out"], p["norm_ffn_g"][l], l)
        x = _ffn(h2, x1, seqs, p["ffn_w_up"], p["ffn_conv_w"][l], p["ffn_conv_b"][l], p["ffn_w_down"],
                 p["final_norm_g"], l, final_norm=(l == depth - 1))
    return x


def kernel(x_prompt, x_sample, norm_mix_g, w_in, conv_dw_w, conv_dw_b, conv_ln_g, conv_ln_b, na_rel_bias, lru_conv_w, lru_conv_b, lru_wa, lru_ba, lru_wx, lru_bx, lru_lambda, w_out, norm_ffn_g, ffn_w_up, ffn_conv_w, ffn_conv_b, ffn_w_down, final_norm_g):
    depth = w_in.shape[0]
    p = dict(
        norm_mix_g=norm_mix_g, conv_dw_w=conv_dw_w, conv_dw_b=conv_dw_b, conv_ln_g=conv_ln_g, conv_ln_b=conv_ln_b,
        lru_conv_w=lru_conv_w, lru_conv_b=lru_conv_b, lru_ba=lru_ba, lru_bx=lru_bx, lru_lambda=lru_lambda,
        norm_ffn_g=norm_ffn_g, ffn_conv_w=ffn_conv_w, ffn_conv_b=ffn_conv_b, final_norm_g=final_norm_g,
        w_in=w_in.astype(BF16), w_out=w_out.astype(BF16),
        ffn_w_up=ffn_w_up.astype(BF16), ffn_w_down=ffn_w_down.astype(BF16),
        lru_w_gates=jnp.concatenate([lru_wa, lru_wx], axis=-1).astype(BF16),
        bias_pairs=[_bias_pairs(na_rel_bias[l]) for l in range(depth)],
    )
    outs = []
    for xb in (x_prompt, x_sample):
        b, t, _ = xb.shape
        seqs = tuple((s * t, (s + 1) * t) for s in range(b))
        outs.append(_trunk(xb.reshape(b * t, D_MODEL), seqs, p).reshape(xb.shape))
    return tuple(outs)
```
